```python
import jax, jax.numpy as jnp
from jax import lax
import numpy as np

D_MODEL = 2048
BATCH = 4
SEQ = 2048
DEPTH = 4

FOX_W = D_MODEL // 4
FOX_HD = 64
FOX_HEADS = FOX_W // FOX_HD
FOX_BLOCK = 128
GLA_HEADS = 4
GLA_W = D_MODEL // 4
GLA_DV = GLA_W // GLA_HEADS
GLA_DK = GLA_DV // 2
GLA_KW = GLA_HEADS * GLA_DK
GLA_RANK = 16
GLA_TAU = 16.0
GLA_CHUNK = 32
LRU_W = D_MODEL // 2
LRU_BLOCKS = 16
LRU_BW = LRU_W // LRU_BLOCKS
LRU_C = 8.0
CONV_WIDTH = 4
D_MIX = FOX_W + GLA_W + LRU_W
IN_W = 3 * FOX_W + FOX_HEADS + 2 * GLA_KW + 2 * GLA_W + GLA_RANK + 2 * LRU_W
FFN_HIDDEN = -(-8 * D_MODEL // 768) * 256
RMS_EPS = 1e-6

kernel_name = "hymba_fox_gla_rglru_hybrid"


def in_proj_sizes():
    return [FOX_W, FOX_W, FOX_W, FOX_HEADS,
            GLA_KW, GLA_KW, GLA_W, GLA_W, GLA_RANK,
            LRU_W, LRU_W]


def rmsnorm(x, g):
    xf = x.astype(jnp.float32)
    y = xf * lax.rsqrt(jnp.mean(xf * xf, axis=-1, keepdims=True) + RMS_EPS)
    return (y * g.astype(jnp.float32)).astype(x.dtype)


def fox_attention(q, k, v, f_logit, f_bias):
    B, S, H, Dh = q.shape
    c = jnp.cumsum(jax.nn.log_sigmoid((f_logit + f_bias).astype(jnp.float32)), axis=1)
    c = c.transpose(0, 2, 1)
    q, k, v = (t.transpose(0, 2, 1, 3) for t in (q, k, v))
    scale = Dh ** -0.5
    outs = []
    for i in range(S // FOX_BLOCK):
        q0 = i * FOX_BLOCK
        end = q0 + FOX_BLOCK
        s = jnp.einsum('bhqd,bhkd->bhqk', q[:, :, q0:end], k[:, :, :end]).astype(jnp.float32) * scale
        s = s + c[:, :, q0:end, None] - c[:, :, None, :end]
        mask = jnp.arange(end)[None, :] <= (q0 + jnp.arange(FOX_BLOCK))[:, None]
        p = jax.nn.softmax(jnp.where(mask, s, -jnp.inf), axis=-1)
        outs.append(jnp.einsum('bhqk,bhkd->bhqd', p.astype(v.dtype), v[:, :, :end]))
    o = jnp.concatenate(outs, axis=2)
    return o.transpose(0, 2, 1, 3).reshape(B, S, H * Dh)


def gla_chunked(q, k, v, log_alpha):
    B, S, H, Dk = q.shape
    Dv = v.shape[-1]
    C = GLA_CHUNK
    N = S // C

    def chunk(t):
        return t.astype(jnp.float32).reshape(B, N, C, H, t.shape[-1]).transpose(0, 3, 1, 2, 4)

    qc = chunk(q) * (Dk ** -0.5)
    kc = chunk(k)
    vc = chunk(v)
    b = jnp.cumsum(chunk(log_alpha), axis=3)
    causal = jnp.tril(jnp.ones((C, C), dtype=bool))
    diff = b[:, :, :, :, None, :] - b[:, :, :, None, :, :]
    decay = jnp.exp(jnp.where(causal[:, :, None], diff, -jnp.inf))
    A = jnp.einsum('bhntd,bhnsd,bhntsd->bhnts', qc, kc, decay)
    o_intra = jnp.einsum('bhnts,bhnsv->bhntv', A, vc)
    b_last = b[:, :, :, -1, :]
    q_in = qc * jnp.exp(b)
    k_st = kc * jnp.exp(b_last[:, :, :, None, :] - b)
    U = jnp.einsum('bhncd,bhncv->bhndv', k_st, vc)

    def step(state, inp):
        dec, u = inp
        return dec[..., None] * state + u, state

    s0 = jnp.zeros((B, H, Dk, Dv), jnp.float32)
    _, s_prev = lax.scan(step, s0, (jnp.moveaxis(jnp.exp(b_last), 2, 0), jnp.moveaxis(U, 2, 0)))
    s_prev = jnp.moveaxis(s_prev, 0, 2)
    o_inter = jnp.einsum('bhncd,bhndv->bhncv', q_in, s_prev)
    o = o_intra + o_inter
    return o.transpose(0, 2, 3, 1, 4).reshape(B, S, H, Dv)


def causal_depthwise_conv(x, w, b):
    K = w.shape[0]
    S = x.shape[1]
    xp = jnp.pad(x, ((0, 0), (K - 1, 0), (0, 0)))
    out = b
    for j in range(K):
        out = out + xp[:, j:j + S] * w[j]
    return out


def rg_lru(x, w_a, b_a, w_i, b_i, lam):
    B, S, W = x.shape
    xb = x.reshape(B, S, LRU_BLOCKS, LRU_BW)
    r = jax.nn.sigmoid(jnp.einsum('bsnd,nde->bsne', xb, w_a).reshape(B, S, W) + b_a)
    i = jax.nn.sigmoid(jnp.einsum('bsnd,nde->bsne', xb, w_i).reshape(B, S, W) + b_i)
    log_a = (LRU_C * r.astype(jnp.float32)) * jax.nn.log_sigmoid(lam.astype(jnp.float32))
    a = jnp.exp(log_a)
    u = jnp.sqrt(-jnp.expm1(2.0 * log_a)) * (i * x).astype(jnp.float32)

    def step(h, inp):
        a_t, u_t = inp
        h = a_t * h + u_t
        return h, h

    _, hs = lax.scan(step, jnp.zeros((B, W), jnp.float32), (a.swapaxes(0, 1), u.swapaxes(0, 1)))
    return hs.swapaxes(0, 1)


def setup_inputs(seed: int = 0) -> dict:
    key = jax.random.key(seed)
    ks = jax.random.split(key, 24)
    f32 = jnp.float32

    def nrm(k, shape, scale):
        return jax.random.normal(k, shape, f32) * scale

    def gain(k, shape):
        return 1.0 + 0.02 * jax.random.normal(k, shape, f32)

    u = jax.random.uniform(ks[15], (DEPTH, LRU_W), f32, minval=0.9, maxval=0.999)
    a_base = u ** (1.0 / LRU_C)
    lru_lambda = jnp.log(a_base) - jnp.log1p(-a_base)
    return {
        "x": nrm(ks[0], (BATCH, SEQ, D_MODEL), 1.0),
        "norm_mix": gain(ks[1], (DEPTH, D_MODEL)),
        "w_in": nrm(ks[2], (DEPTH, D_MODEL, IN_W), D_MODEL ** -0.5),
        "fox_f_bias": 2.0 + 0.1 * jax.random.normal(ks[3], (DEPTH, FOX_HEADS), f32),
        "fox_out_norm": gain(ks[4], (DEPTH, FOX_W)),
        "gla_gate_w2": nrm(ks[5], (DEPTH, GLA_RANK, GLA_KW), GLA_RANK ** -0.5),
        "gla_gate_bias": nrm(ks[6], (DEPTH, GLA_KW), 0.1),
        "gla_head_norm": gain(ks[7], (DEPTH, GLA_DV)),
        "conv_w": nrm(ks[8], (DEPTH, CONV_WIDTH, LRU_W), CONV_WIDTH ** -0.5),
        "conv_b": nrm(ks[9], (DEPTH, LRU_W), 0.02),
        "lru_w_a": nrm(ks[10], (DEPTH, LRU_BLOCKS, LRU_BW, LRU_BW), LRU_BW ** -0.5),
        "lru_b_a": nrm(ks[11], (DEPTH, LRU_W), 0.1),
        "lru_w_i": nrm(ks[12], (DEPTH, LRU_BLOCKS, LRU_BW, LRU_BW), LRU_BW ** -0.5),
        "lru_b_i": nrm(ks[13], (DEPTH, LRU_W), 0.1),
        "lru_lambda": lru_lambda,
        "lru_out_norm": gain(ks[14], (DEPTH, LRU_W)),
        "w_out": nrm(ks[16], (DEPTH, D_MIX, D_MODEL), D_MIX ** -0.5),
        "norm_ffn": gain(ks[17], (DEPTH, D_MODEL)),
        "w_gate": nrm(ks[18], (DEPTH, D_MODEL, FFN_HIDDEN), D_MODEL ** -0.5),
        "w_up": nrm(ks[19], (DEPTH, D_MODEL, FFN_HIDDEN), D_MODEL ** -0.5),
        "w_down": nrm(ks[20], (DEPTH, FFN_HIDDEN, D_MODEL), FFN_HIDDEN ** -0.5),
        "final_norm": gain(ks[21], (D_MODEL,)),
    }


def reference(x, norm_mix, w_in, fox_f_bias, fox_out_norm, gla_gate_w2, gla_gate_bias,
              gla_head_norm, conv_w, conv_b, lru_w_a, lru_b_a, lru_w_i, lru_b_i, lru_lambda,
              lru_out_norm, w_out, norm_ffn, w_gate, w_up, w_down, final_norm):
    B, S, _ = x.shape
    offsets = [int(o) for o in np.cumsum(in_proj_sizes())[:-1]]
    for l in range(DEPTH):
        h = rmsnorm(x, norm_mix[l])
        proj = h @ w_in[l]
        fq, fk, fv, ff, gq, gk, gv, gg, gr, lg, lx = jnp.split(proj, offsets, axis=-1)
        fox = fox_attention(fq.reshape(B, S, FOX_HEADS, FOX_HD), fk.reshape(B, S, FOX_HEADS, FOX_HD),
                            fv.reshape(B, S, FOX_HEADS, FOX_HD), ff, fox_f_bias[l])
        fox = rmsnorm(fox, fox_out_norm[l])
        log_alpha = jax.nn.log_sigmoid((gr @ gla_gate_w2[l] + gla_gate_bias[l]).astype(jnp.float32)) / GLA_TAU
        gla = gla_chunked(gq.reshape(B, S, GLA_HEADS, GLA_DK), gk.reshape(B, S, GLA_HEADS, GLA_DK),
                          gv.reshape(B, S, GLA_HEADS, GLA_DV), log_alpha.reshape(B, S, GLA_HEADS, GLA_DK))
        gla = rmsnorm(gla, gla_head_norm[l]) * jax.nn.silu(gg.reshape(B, S, GLA_HEADS, GLA_DV).astype(jnp.float32))
        gla = gla.reshape(B, S, GLA_W).astype(h.dtype)
        lru_in = causal_depthwise_conv(lx, conv_w[l], conv_b[l])
        lru = rg_lru(lru_in, lru_w_a[l], lru_b_a[l], lru_w_i[l], lru_b_i[l], lru_lambda[l])
        lru = rmsnorm((lru * jax.nn.gelu(lg.astype(jnp.float32))).astype(h.dtype), lru_out_norm[l])
        mix = jnp.concatenate([fox.astype(h.dtype), gla, lru], axis=-1) @ w_out[l]
        x = x + mix.astype(x.dtype)
        h = rmsnorm(x, norm_ffn[l])
        ffn = (jax.nn.silu(h @ w_gate[l]) * (h @ w_up[l])) @ w_down[l]
        x = x + ffn.astype(x.dtype)
    return rmsnorm(x, final_norm)
```

```python
import functools

import jax
import jax.numpy as jnp
from jax import lax
from jax.experimental import pallas as pl
from jax.experimental.pallas import tpu as pltpu

F32 = jnp.float32
BF16 = jnp.bfloat16

FOX_HEADS = 8
FOX_HD = 64
FOX_W = FOX_HEADS * FOX_HD
GLA_HEADS = 4
GLA_DK = 64
GLA_DV = 128
GLA_KW = GLA_HEADS * GLA_DK
GLA_W = GLA_HEADS * GLA_DV
GLA_RANK = 16
GLA_TAU = 16.0
LRU_BLOCKS = 16
LRU_BW = 64
LRU_W = LRU_BLOCKS * LRU_BW
LRU_C = 8.0
CONV_WIDTH = 4
RMS_EPS = 1e-6

LANES = 128
SUBLANES = 8
VMEM_LIMIT = 56 * 1024 * 1024

MAIN_FQ = 0
MAIN_FK = FOX_W
MAIN_FV = 2 * FOX_W
MAIN_GQ = 3 * FOX_W
MAIN_GK = MAIN_GQ + GLA_KW
MAIN_GV = MAIN_GK + GLA_KW
MAIN_GG = MAIN_GV + GLA_W
MAIN_LG = MAIN_GG + GLA_W
MAIN_LX = MAIN_LG + LRU_W
MAIN_W = MAIN_LX + LRU_W
SMALL_FF = 0
SMALL_GR = FOX_HEADS


def _dot(a, b):
    return jnp.dot(a, b, preferred_element_type=F32)


def _dot_nt(a, b):
    return lax.dot_general(a, b, (((1,), (1,)), ((), ())), preferred_element_type=F32)


def _dot_tn(a, b):
    return lax.dot_general(a, b, (((0,), (0,)), ((), ())), preferred_element_type=F32)


def _split3(x):
    hi = x.astype(BF16)
    r1 = x - hi.astype(F32)
    mid = r1.astype(BF16)
    lo = (r1 - mid.astype(F32)).astype(BF16)
    return hi, mid, lo


def _tri_cumsum(tri, x):
    hi, mid, lo = _split3(x)
    return _dot(tri, hi) + _dot(tri, mid) + _dot(tri, lo)


def _lower_tri(n):
    r = lax.broadcasted_iota(jnp.int32, (n, n), 0)
    c = lax.broadcasted_iota(jnp.int32, (n, n), 1)
    return jnp.where(r >= c, 1.0, 0.0).astype(BF16)


def _log_sigmoid(x):
    return jnp.minimum(x, 0.0) - jnp.log1p(jnp.exp(-jnp.abs(x)))


def _rms_scale(x):
    return lax.rsqrt(jnp.mean(x * x, axis=-1, keepdims=True) + RMS_EPS)


def _params(*sem):
    return pltpu.CompilerParams(dimension_semantics=sem, vmem_limit_bytes=VMEM_LIMIT)


def _inproj_kernel(x_ref, g_ref, wm_ref, ws_ref, main_ref, small_ref, xn_ref):
    @pl.when(pl.program_id(1) == 0)
    def _():
        x = x_ref[...]
        xn = (x * _rms_scale(x) * g_ref[...]).astype(BF16)
        xn_ref[...] = xn
        small_ref[...] = _dot(xn, ws_ref[...])

    main_ref[...] = _dot(xn_ref[...], wm_ref[...]).astype(BF16)


def _inproj(x2, g, wm, ws, tm, tn):
    m, d = x2.shape
    n = wm.shape[1]
    return pl.pallas_call(
        _inproj_kernel,
        grid=(m // tm, n // tn),
        in_specs=[
            pl.BlockSpec((tm, d), lambda i, j: (i, 0)),
            pl.BlockSpec((1, d), lambda i, j: (0, 0)),
            pl.BlockSpec((d, tn), lambda i, j: (0, j)),
            pl.BlockSpec((d, LANES), lambda i, j: (0, 0)),
        ],
        out_specs=[
            pl.BlockSpec((tm, tn), lambda i, j: (i, j)),
            pl.BlockSpec((tm, LANES), lambda i, j: (i, 0)),
        ],
        out_shape=[
            jax.ShapeDtypeStruct((m, n), BF16),
            jax.ShapeDtypeStruct((m, LANES), F32),
        ],
        scratch_shapes=[pltpu.VMEM((tm, d), BF16)],
        compiler_params=_params("parallel", "arbitrary"),
        name="in_proj",
    )(x2, g, wm, ws)


def _foxprep_kernel(small_ref, bias_ref, ccol_ref, crow_ref, *, blk):
    s = small_ref.shape[0]
    tri = _lower_tri(blk)
    carry = jnp.zeros((1, LANES), F32)
    for i in range(s // blk):
        x = small_ref[i * blk:(i + 1) * blk, :] + bias_ref[...]
        c = _tri_cumsum(tri, _log_sigmoid(x)) + carry
        carry = c[blk - 1:blk, :]
        ccol_ref[i * blk:(i + 1) * blk, :] = c
        crow_ref[:, i * blk:(i + 1) * blk] = c.T[0:FOX_HEADS, :]


def _foxprep(small3, bias_pad):
    b, s, _ = small3.shape
    blk = min(256, s)
    return pl.pallas_call(
        functools.partial(_foxprep_kernel, blk=blk),
        grid=(b,),
        in_specs=[
            pl.BlockSpec((None, s, LANES), lambda i: (i, 0, 0)),
            pl.BlockSpec((1, LANES), lambda i: (0, 0)),
        ],
        out_specs=[
            pl.BlockSpec((None, s, LANES), lambda i: (i, 0, 0)),
            pl.BlockSpec((None, FOX_HEADS, s), lambda i: (i, 0, 0)),
        ],
        out_shape=[
            jax.ShapeDtypeStruct((b, s, LANES), F32),
            jax.ShapeDtypeStruct((b, FOX_HEADS, s), F32),
        ],
        compiler_params=_params("parallel"),
        name="fox_prep",
    )(small3, bias_pad)


def _fox_kernel(q_ref, k_ref, v_ref, ccol_ref, crow_ref, o_ref, *, tq, tk):
    s = q_ref.shape[0]
    pair = pl.program_id(1)
    lane = lax.broadcasted_iota(jnp.int32, (1, LANES), 1)
    first = lane < FOX_HD
    qpos_l = lax.broadcasted_iota(jnp.int32, (tq, tk), 0)
    kpos_l = lax.broadcasted_iota(jnp.int32, (tq, tk), 1)

    def q_block(qb, _):
        q0 = pl.multiple_of(qb * tq, tq)
        q = q_ref[pl.ds(q0, tq), :] * (FOX_HD ** -0.5)
        cc = ccol_ref[pl.ds(q0, tq), :]
        diag = (qb * tq) // tk
        outs = []
        for hh in range(2):
            head = 2 * pair + hh
            sel = first if hh == 0 else jnp.logical_not(first)
            qm = jnp.where(sel, q, jnp.zeros_like(q))
            ct = jnp.sum(jnp.where(lane == head, cc, 0.0), axis=1, keepdims=True)

            def chunk(j, carry, masked):
                m, l, acc = carry
                k0 = pl.multiple_of(j * tk, tk)
                kk = k_ref[pl.ds(k0, tk), :]
                vv = v_ref[pl.ds(k0, tk), :]
                cs = crow_ref[pl.ds(head, 1), pl.ds(k0, tk)]
                sc = _dot_nt(qm, kk) + ct - cs
                if masked:
                    sc = jnp.where(kpos_l + k0 <= qpos_l + q0, sc, -jnp.inf)
                m_new = jnp.maximum(m, jnp.max(sc, axis=1, keepdims=True))
                alpha = jnp.exp(m - m_new)
                p = jnp.exp(sc - m_new)
                l = alpha * l + jnp.sum(p, axis=1, keepdims=True)
                acc = alpha * acc + _dot(p.astype(BF16), vv)
                return m_new, l, acc

            init = (jnp.full((tq, 1), -jnp.inf, F32), jnp.zeros((tq, 1), F32),
                    jnp.zeros((tq, LANES), F32))
            carry = lax.fori_loop(0, diag, functools.partial(chunk, masked=False), init)
            m, l, acc = chunk(diag, carry, True)
            outs.append(acc / l)
        o_ref[pl.ds(q0, tq), :] = jnp.where(first, outs[0], outs[1])
        return 0

    lax.fori_loop(0, s // tq, q_block, 0)


def _fox(main3, ccol, crow, tq=128, tk=256):
    b, s, _ = main3.shape
    tq, tk = min(tq, s), min(tk, s)
    npair = FOX_W // LANES
    return pl.pallas_call(
        functools.partial(_fox_kernel, tq=tq, tk=tk),
        grid=(b, npair),
        in_specs=[
            pl.BlockSpec((None, s, LANES), lambda i, p: (i, 0, MAIN_FQ // LANES + p)),
            pl.BlockSpec((None, s, LANES), lambda i, p: (i, 0, MAIN_FK // LANES + p)),
            pl.BlockSpec((None, s, LANES), lambda i, p: (i, 0, MAIN_FV // LANES + p)),
            pl.BlockSpec((None, s, LANES), lambda i, p: (i, 0, 0)),
            pl.BlockSpec((None, FOX_HEADS, s), lambda i, p: (i, 0, 0)),
        ],
        out_specs=pl.BlockSpec((None, s, LANES), lambda i, p: (i, 0, p)),
        out_shape=jax.ShapeDtypeStruct((b, s, FOX_W), F32),
        compiler_params=_params("parallel", "arbitrary"),
        name="fox",
    )(main3, main3, main3, ccol, crow)


def _gla_kernel(q_ref, k_ref, v_ref, g_ref, small_ref, w2_ref, gb_ref, hn_ref, o_ref, state_ref):
    L = q_ref.shape[0]

    @pl.when(pl.program_id(1) == 0)
    def _():
        state_ref[...] = jnp.zeros_like(state_ref)

    sm_hi, sm_mid, _ = _split3(small_ref[...])
    w_hi, w_mid, _ = _split3(w2_ref[...])
    z = _dot(sm_hi, w_hi) + _dot(sm_mid, w_hi) + _dot(sm_hi, w_mid) + gb_ref[...]
    la = _log_sigmoid(z) * (1.0 / GLA_TAU)
    bcum = _tri_cumsum(_lower_tri(L), la)
    b_last = bcum[L - 1:L, :]
    b_mid = bcum[L // 2 - 1:L // 2, :]

    q = q_ref[...].astype(F32) * (GLA_DK ** -0.5)
    k = k_ref[...].astype(F32)
    q_in = (q * jnp.exp(bcum)).astype(BF16)
    q_mid = q * jnp.exp(bcum - b_mid)
    k_mid = (k * jnp.exp(b_mid - bcum)).astype(BF16)
    k_st = (k * jnp.exp(b_last - bcum)).astype(BF16)
    v = v_ref[...]

    state_t = state_ref[...]
    o_inter = _dot_nt(q_in, state_t.astype(BF16))

    lane = lax.broadcasted_iota(jnp.int32, (1, GLA_KW), 1)
    r = lax.broadcasted_iota(jnp.int32, (L, L), 0)
    c = lax.broadcasted_iota(jnp.int32, (L, L), 1)
    causal = r >= c
    hn = hn_ref[...]
    outs = []
    for h in range(GLA_HEADS):
        in_head = (lane >= h * GLA_DK) & (lane < (h + 1) * GLA_DK)
        qh = jnp.where(in_head, q_mid, 0.0).astype(BF16)
        a = jnp.where(causal, _dot_nt(qh, k_mid), 0.0)
        vs = slice(h * GLA_DV, (h + 1) * GLA_DV)
        o = _dot(a.astype(BF16), v[:, vs]) + o_inter[:, vs]
        y = o * _rms_scale(o) * hn
        gate = g_ref[:, vs].astype(F32)
        outs.append(y * (gate * jax.nn.sigmoid(gate)))
    o_ref[...] = jnp.concatenate(outs, axis=1).astype(BF16)

    upd = _dot_tn(v, k_st)
    rr = lax.broadcasted_iota(jnp.int32, upd.shape, 0) // GLA_DV
    cc = lax.broadcasted_iota(jnp.int32, upd.shape, 1) // GLA_DK
    state_ref[...] = jnp.where(rr == cc, state_t * jnp.exp(b_last) + upd, 0.0)


def _gla(main3, small3, w2_pad, gbias, hnorm, chunk=128):
    b, s, _ = main3.shape
    L = min(chunk, s)
    return pl.pallas_call(
        _gla_kernel,
        grid=(b, s // L),
        in_specs=[
            pl.BlockSpec((None, L, GLA_KW), lambda i, c: (i, c, MAIN_GQ // GLA_KW)),
            pl.BlockSpec((None, L, GLA_KW), lambda i, c: (i, c, MAIN_GK // GLA_KW)),
            pl.BlockSpec((None, L, GLA_W), lambda i, c: (i, c, MAIN_GV // GLA_W)),
            pl.BlockSpec((None, L, GLA_W), lambda i, c: (i, c, MAIN_GG // GLA_W)),
            pl.BlockSpec((None, L, LANES), lambda i, c: (i, c, 0)),
            pl.BlockSpec((LANES, GLA_KW), lambda i, c: (0, 0)),
            pl.BlockSpec((1, GLA_KW), lambda i, c: (0, 0)),
            pl.BlockSpec((1, GLA_DV), lambda i, c: (0, 0)),
        ],
        out_specs=pl.BlockSpec((None, L, GLA_W), lambda i, c: (i, c, 0)),
        out_shape=jax.ShapeDtypeStruct((b, s, GLA_W), BF16),
        scratch_shapes=[pltpu.VMEM((GLA_W, GLA_KW), F32)],
        compiler_params=_params("parallel", "arbitrary"),
        name="gla",
    )(main3, main3, main3, main3, small3, w2_pad, gbias, hnorm)


def _gelu_tanh(x):
    return 0.5 * x * (1.0 + jnp.tanh(0.7978845608028654 * (x + 0.044715 * (x * x * x))))


def _lru_kernel(lg_ref, lx_ref, cw_ref, cb_ref, wab_ref, ba_ref, bi_ref, lam_ref, on_ref,
                o_ref, prev_ref, h_ref, a_ref, u_ref):
    T = lx_ref.shape[0]

    @pl.when(pl.program_id(1) == 0)
    def _():
        prev_ref[...] = jnp.zeros_like(prev_ref)
        h_ref[...] = jnp.zeros_like(h_ref)

    x = lx_ref[...].astype(F32)
    xcat = jnp.concatenate([prev_ref[...], x], axis=0)
    cw = cw_ref[...]
    xc = cb_ref[...] + cw[CONV_WIDTH - 1:CONV_WIDTH, :] * x
    for d in range(1, CONV_WIDTH):
        shifted = pltpu.roll(xcat, d, 0)[SUBLANES:, :]
        xc = xc + cw[CONV_WIDTH - 1 - d:CONV_WIDTH - d, :] * shifted
    prev_ref[...] = x[T - SUBLANES:, :]

    xcb = xc.astype(BF16)
    zr, zi = [], []
    for g in range(LRU_W // LANES):
        z = _dot(xcb[:, g * LANES:(g + 1) * LANES], wab_ref[g])
        zr.append(z[:, :LANES])
        zi.append(z[:, LANES:])
    r = jax.nn.sigmoid(jnp.concatenate(zr, axis=1) + ba_ref[...])
    ig = jax.nn.sigmoid(jnp.concatenate(zi, axis=1) + bi_ref[...])
    log_a = (LRU_C * r) * _log_sigmoid(lam_ref[...])
    a = jnp.exp(log_a)
    a_ref[...] = a
    u_ref[...] = jnp.sqrt(1.0 - a * a) * (ig * xc)

    def step(t, h):
        h = a_ref[pl.ds(t, 1), :] * h + u_ref[pl.ds(t, 1), :]
        u_ref[pl.ds(t, 1), :] = h
        return h

    h = lax.fori_loop(0, T, step, h_ref[...], unroll=8)
    h_ref[...] = h

    y = u_ref[...] * _gelu_tanh(lg_ref[...].astype(F32))
    o_ref[...] = (y * _rms_scale(y) * on_ref[...]).astype(BF16)


def _lru(main3, cw, cb, wab, ba, bi, lam, onorm, tile=256):
    b, s, _ = main3.shape
    T = min(tile, s)
    vec = pl.BlockSpec((1, LRU_W), lambda i, t: (0, 0))
    return pl.pallas_call(
        _lru_kernel,
        grid=(b, s // T),
        in_specs=[
            pl.BlockSpec((None, T, LRU_W), lambda i, t: (i, t, MAIN_LG // LRU_W)),
            pl.BlockSpec((None, T, LRU_W), lambda i, t: (i, t, MAIN_LX // LRU_W)),
            pl.BlockSpec((CONV_WIDTH, LRU_W), lambda i, t: (0, 0)),
            vec,
            pl.BlockSpec((LRU_W // LANES, LANES, 2 * LANES), lambda i, t: (0, 0, 0)),
            vec, vec, vec, vec,
        ],
        out_specs=pl.BlockSpec((None, T, LRU_W), lambda i, t: (i, t, 0)),
        out_shape=jax.ShapeDtypeStruct((b, s, LRU_W), BF16),
        scratch_shapes=[
            pltpu.VMEM((SUBLANES, LRU_W), F32),
            pltpu.VMEM((1, LRU_W), F32),
            pltpu.VMEM((T, LRU_W), F32),
            pltpu.VMEM((T, LRU_W), F32),
        ],
        compiler_params=_params("parallel", "arbitrary"),
        name="lru",
    )(main3, main3, cw, cb, wab, ba, bi, lam, onorm)


def _outproj_kernel(fox_ref, fn_ref, gla_ref, lru_ref, w_ref, x_ref, o_ref, mix_ref):
    @pl.when(pl.program_id(1) == 0)
    def _():
        f = fox_ref[...]
        mix_ref[:, 0:FOX_W] = (f * _rms_scale(f) * fn_ref[...]).astype(BF16)
        mix_ref[:, FOX_W:FOX_W + GLA_W] = gla_ref[...]
        mix_ref[:, FOX_W + GLA_W:] = lru_ref[...]

    o_ref[...] = x_ref[...] + _dot(mix_ref[...], w_ref[...])


def _outproj(fox2, fnorm, gla2, lru2, w, x2, tm, tn):
    m, d = x2.shape
    dmix = w.shape[0]
    return pl.pallas_call(
        _outproj_kernel,
        grid=(m // tm, d // tn),
        in_specs=[
            pl.BlockSpec((tm, FOX_W), lambda i, j: (i, 0)),
            pl.BlockSpec((1, FOX_W), lambda i, j: (0, 0)),
            pl.BlockSpec((tm, GLA_W), lambda i, j: (i, 0)),
            pl.BlockSpec((tm, LRU_W), lambda i, j: (i, 0)),
            pl.BlockSpec((dmix, tn), lambda i, j: (0, j)),
            pl.BlockSpec((tm, tn), lambda i, j: (i, j)),
        ],
        out_specs=pl.BlockSpec((tm, tn), lambda i, j: (i, j)),
        out_shape=jax.ShapeDtypeStruct((m, d), F32),
        scratch_shapes=[pltpu.VMEM((tm, dmix), BF16)],
        compiler_params=_params("parallel", "arbitrary"),
        name="out_proj",
    )(fox2, fnorm, gla2, lru2, w, x2)


def _ffn_kernel(x_ref, g_ref, wg_ref, wu_ref, wd_ref, o_ref, xn_ref, acc_ref):
    j = pl.program_id(1)

    @pl.when(j == 0)
    def _():
        x = x_ref[...]
        xn_ref[...] = (x * _rms_scale(x) * g_ref[...]).astype(BF16)
        acc_ref[...] = jnp.zeros_like(acc_ref)

    xn = xn_ref[...]
    gate = _dot(xn, wg_ref[...])
    up = _dot(xn, wu_ref[...])
    hid = (gate * jax.nn.sigmoid(gate) * up).astype(BF16)
    acc_ref[...] += _dot(hid, wd_ref[...])

    @pl.when(j == pl.num_programs(1) - 1)
    def _():
        o_ref[...] = x_ref[...] + acc_ref[...]


def _ffn(x2, g, wg, wu, wd, tm, th):
    m, d = x2.shape
    hdim = wg.shape[1]
    return pl.pallas_call(
        _ffn_kernel,
        grid=(m // tm, hdim // th),
        in_specs=[
            pl.BlockSpec((tm, d), lambda i, j: (i, 0)),
            pl.BlockSpec((1, d), lambda i, j: (0, 0)),
            pl.BlockSpec((d, th), lambda i, j: (0, j)),
            pl.BlockSpec((d, th), lambda i, j: (0, j)),
            pl.BlockSpec((th, d), lambda i, j: (j, 0)),
        ],
        out_specs=pl.BlockSpec((tm, d), lambda i, j: (i, 0)),
        out_shape=jax.ShapeDtypeStruct((m, d), F32),
        scratch_shapes=[pltpu.VMEM((tm, d), BF16), pltpu.VMEM((tm, d), F32)],
        compiler_params=_params("parallel", "arbitrary"),
        name="ffn",
    )(x2, g, wg, wu, wd)


def _final_kernel(x_ref, g_ref, o_ref):
    x = x_ref[...]
    o_ref[...] = x * _rms_scale(x) * g_ref[...]


def _final_norm(x2, g, tm):
    m, d = x2.shape
    return pl.pallas_call(
        _final_kernel,
        grid=(m // tm,),
        in_specs=[pl.BlockSpec((tm, d), lambda i: (i, 0)), pl.BlockSpec((1, d), lambda i: (0, 0))],
        out_specs=pl.BlockSpec((tm, d), lambda i: (i, 0)),
        out_shape=jax.ShapeDtypeStruct((m, d), F32),
        compiler_params=_params("parallel"),
        name="final_norm",
    )(x2, g)


def _tile(n, want):
    t = min(want, n)
    while n % t:
        t //= 2
    return t


def _prep_layer_weights(w_in_l, w2_l, w_a_l, w_i_l):
    d = w_in_l.shape[0]
    sizes = [FOX_W, FOX_W, FOX_W, FOX_HEADS, GLA_KW, GLA_KW, GLA_W, GLA_W, GLA_RANK, LRU_W, LRU_W]
    offs = [0]
    for sz in sizes:
        offs.append(offs[-1] + sz)
    col = lambda i: w_in_l[:, offs[i]:offs[i + 1]]
    wm = jnp.concatenate([col(0), col(1), col(2), col(4), col(5), col(6), col(7), col(9), col(10)],
                         axis=1).astype(BF16)
    ws = jnp.concatenate(
        [col(3), col(8), jnp.zeros((d, LANES - FOX_HEADS - GLA_RANK), w_in_l.dtype)], axis=1).astype(BF16)
    w2_pad = jnp.zeros((LANES, GLA_KW), F32).at[SMALL_GR:SMALL_GR + GLA_RANK, :].set(w2_l)
    def pair_bd(w):
        w = w.reshape(LRU_W // LANES, 2, LRU_BW, LRU_BW)
        z = jnp.zeros((LRU_W // LANES, LRU_BW, LRU_BW), w.dtype)
        top = jnp.concatenate([w[:, 0], z], axis=2)
        bot = jnp.concatenate([z, w[:, 1]], axis=2)
        return jnp.concatenate([top, bot], axis=1)
    wab = jnp.concatenate([pair_bd(w_a_l), pair_bd(w_i_l)], axis=2).astype(BF16)
    return wm, ws, w2_pad, wab


def kernel(x, norm_mix, w_in, fox_f_bias, fox_out_norm, gla_gate_w2, gla_gate_bias, gla_head_norm,
           conv_w, conv_b, lru_w_a, lru_b_a, lru_w_i, lru_b_i, lru_lambda, lru_out_norm, w_out,
           norm_ffn, w_gate, w_up, w_down, final_norm):
    b, s, d = x.shape
    depth = w_in.shape[0]
    m = b * s
    x2 = x.reshape(m, d)
    tm_big = _tile(m, 1024)
    row = lambda v: v.reshape(1, -1)

    for l in range(depth):
        wm, ws, w2_pad, wab = _prep_layer_weights(w_in[l], gla_gate_w2[l], lru_w_a[l], lru_w_i[l])
        main, small = _inproj(x2, row(norm_mix[l]), wm, ws, tm_big, _tile(MAIN_W, 1024))
        main3 = main.reshape(b, s, MAIN_W)
        small3 = small.reshape(b, s, LANES)

        fbias = jnp.zeros((1, LANES), F32).at[0, :FOX_HEADS].set(fox_f_bias[l])
        ccol, crow = _foxprep(small3, fbias)
        fox = _fox(main3, ccol, crow)

        gla = _gla(main3, small3, w2_pad, row(gla_gate_bias[l]), row(gla_head_norm[l]))

        lru = _lru(main3, conv_w[l], row(conv_b[l]), wab, row(lru_b_a[l]), row(lru_b_i[l]),
                   row(lru_lambda[l]), row(lru_out_norm[l]))

        x2 = _outproj(fox.reshape(m, FOX_W), row(fox_out_norm[l]), gla.reshape(m, GLA_W),
                      lru.reshape(m, LRU_W), w_out[l].astype(BF16), x2, tm_big, _tile(d, 1024))

        x2 = _ffn(x2, row(norm_ffn[l]), w_gate[l].astype(BF16), w_up[l].astype(BF16),
                  w_down[l].astype(BF16), _tile(m, 512), _tile(w_gate.shape[2], 512))

    return _final_norm(x2, row(final_norm), _tile(m, 512)).reshape(b, s, d)
```

```python
import functools

import jax
import jax.numpy as jnp
from jax import lax
from jax.experimental import pallas as pl
from jax.experimental.pallas import tpu as pltpu

F32 = jnp.float32
BF16 = jnp.bfloat16

FOX_HEADS = 8
FOX_HD = 64
FOX_W = FOX_HEADS * FOX_HD
GLA_HEADS = 4
GLA_DK = 64
GLA_DV = 128
GLA_KW = GLA_HEADS * GLA_DK
GLA_W = GLA_HEADS * GLA_DV
GLA_RANK = 16
GLA_TAU = 16.0
LRU_BLOCKS = 16
LRU_BW = 64
LRU_W = LRU_BLOCKS * LRU_BW
LRU_C = 8.0
CONV_WIDTH = 4
RMS_EPS = 1e-6

LANES = 128
SUBLANES = 8
VMEM_LIMIT = 56 * 1024 * 1024

MAIN_FQ = 0
MAIN_FK = FOX_W
MAIN_FV = 2 * FOX_W
MAIN_GQ = 3 * FOX_W
MAIN_GK = MAIN_GQ + GLA_KW
MAIN_GV = MAIN_GK + GLA_KW
MAIN_GG = MAIN_GV + GLA_W
MAIN_LG = MAIN_GG + GLA_W
MAIN_LX = MAIN_LG + LRU_W
MAIN_W = MAIN_LX + LRU_W
SMALL_FF = 0
SMALL_GR = FOX_HEADS


def _dot(a, b):
    return jnp.dot(a, b, preferred_element_type=F32)


def _dot_nt(a, b):
    return lax.dot_general(a, b, (((1,), (1,)), ((), ())), preferred_element_type=F32)


def _dot_tn(a, b):
    return lax.dot_general(a, b, (((0,), (0,)), ((), ())), preferred_element_type=F32)


def _split3(x):
    hi = x.astype(BF16)
    r1 = x - hi.astype(F32)
    mid = r1.astype(BF16)
    lo = (r1 - mid.astype(F32)).astype(BF16)
    return hi, mid, lo


def _tri_cumsum(tri, x):
    hi, mid, lo = _split3(x)
    return _dot(tri, hi) + _dot(tri, mid) + _dot(tri, lo)


def _lower_tri(n):
    r = lax.broadcasted_iota(jnp.int32, (n, n), 0)
    c = lax.broadcasted_iota(jnp.int32, (n, n), 1)
    return jnp.where(r >= c, 1.0, 0.0).astype(BF16)


def _log_sigmoid(x):
    return jnp.minimum(x, 0.0) - jnp.log1p(jnp.exp(-jnp.abs(x)))


def _rms_scale(x):
    return lax.rsqrt(jnp.mean(x * x, axis=-1, keepdims=True) + RMS_EPS)


def _params(*sem):
    return pltpu.CompilerParams(dimension_semantics=sem, vmem_limit_bytes=VMEM_LIMIT)


def _inproj_kernel(x_ref, g_ref, wm_ref, ws_ref, main_ref, small_ref, xn_ref):
    @pl.when(pl.program_id(1) == 0)
    def _():
        x = x_ref[...]
        xn = (x * _rms_scale(x) * g_ref[...]).astype(BF16)
        xn_ref[...] = xn
        small_ref[...] = _dot(xn, ws_ref[...])

    main_ref[...] = _dot(xn_ref[...], wm_ref[...]).astype(BF16)


def _inproj(x2, g, wm, ws, tm, tn):
    m, d = x2.shape
    n = wm.shape[1]
    return pl.pallas_call(
        _inproj_kernel,
        grid=(m // tm, n // tn),
        in_specs=[
            pl.BlockSpec((tm, d), lambda i, j: (i, 0)),
            pl.BlockSpec((1, d), lambda i, j: (0, 0)),
            pl.BlockSpec((d, tn), lambda i, j: (0, j)),
            pl.BlockSpec((d, LANES), lambda i, j: (0, 0)),
        ],
        out_specs=[
            pl.BlockSpec((tm, tn), lambda i, j: (i, j)),
            pl.BlockSpec((tm, LANES), lambda i, j: (i, 0)),
        ],
        out_shape=[
            jax.ShapeDtypeStruct((m, n), BF16),
            jax.ShapeDtypeStruct((m, LANES), F32),
        ],
        scratch_shapes=[pltpu.VMEM((tm, d), BF16)],
        compiler_params=_params("parallel", "arbitrary"),
        name="in_proj",
    )(x2, g, wm, ws)


def _foxprep_kernel(small_ref, bias_ref, ccol_ref, crow_ref, *, blk):
    s = small_ref.shape[0]
    tri = _lower_tri(blk)
    carry = jnp.zeros((1, LANES), F32)
    for i in range(s // blk):
        x = small_ref[i * blk:(i + 1) * blk, :] + bias_ref[...]
        c = _tri_cumsum(tri, _log_sigmoid(x)) + carry
        carry = c[blk - 1:blk, :]
        ccol_ref[i * blk:(i + 1) * blk, :] = c
        crow_ref[:, i * blk:(i + 1) * blk] = c.T[0:FOX_HEADS, :]


def _foxprep(small3, bias_pad):
    b, s, _ = small3.shape
    blk = min(256, s)
    return pl.pallas_call(
        functools.partial(_foxprep_kernel, blk=blk),
        grid=(b,),
        in_specs=[
            pl.BlockSpec((None, s, LANES), lambda i: (i, 0, 0)),
            pl.BlockSpec((1, LANES), lambda i: (0, 0)),
        ],
        out_specs=[
            pl.BlockSpec((None, s, LANES), lambda i: (i, 0, 0)),
            pl.BlockSpec((None, FOX_HEADS, s), lambda i: (i, 0, 0)),
        ],
        out_shape=[
            jax.ShapeDtypeStruct((b, s, LANES), F32),
            jax.ShapeDtypeStruct((b, FOX_HEADS, s), F32),
        ],
        compiler_params=_params("parallel"),
        name="fox_prep",
    )(small3, bias_pad)


def _fox_kernel(q_ref, k_ref, v_ref, ccol_ref, crow_ref, o_ref, vt_ref, csb_ref, *, tq):
    s = q_ref.shape[0]
    pair = pl.program_id(1)
    lane = lax.broadcasted_iota(jnp.int32, (1, LANES), 1)
    first = lane < FOX_HD

    vt_ref[...] = v_ref[...].astype(F32).T.astype(BF16)
    cc = ccol_ref[...]
    for hh in range(2):
        col = jnp.sum(jnp.where(lane == 2 * pair + hh, cc, 0.0), axis=1, keepdims=True)
        csb_ref[hh] = jnp.broadcast_to(col, (s, LANES))

    kpos = lax.broadcasted_iota(jnp.int32, (tq, tq), 0)
    qpos = lax.broadcasted_iota(jnp.int32, (tq, tq), 1)
    causal = kpos <= qpos
    top_half = lax.broadcasted_iota(jnp.int32, (LANES, 1), 0) < FOX_HD

    for i in range(s // tq):
        q0, p_len = i * tq, (i + 1) * tq
        q = q_ref[q0:p_len, :] * (FOX_HD ** -0.5)
        o_heads = []
        for hh in range(2):
            sel = first if hh == 0 else jnp.logical_not(first)
            qm = jnp.where(sel, q, jnp.zeros_like(q))
            ct = crow_ref[pl.ds(2 * pair + hh, 1), q0:p_len]
            cs = csb_ref[hh, 0:p_len, :]
            st = (_dot_nt(k_ref[0:p_len, :], qm) + ct) - jnp.concatenate([cs] * (tq // LANES), axis=1)
            diag = jnp.where(causal, st[q0:p_len, :], -jnp.inf)
            st = diag if i == 0 else jnp.concatenate([st[0:q0, :], diag], axis=0)
            m = jnp.max(st, axis=0, keepdims=True)
            p = jnp.exp(st - m)
            l = jnp.sum(p, axis=0, keepdims=True)
            ot = _dot(vt_ref[:, 0:p_len], p.astype(BF16))
            o_heads.append(ot * (1.0 / l))
        o_ref[q0:p_len, :] = jnp.where(top_half, o_heads[0], o_heads[1]).T


def _fox(main3, ccol, crow, tq=256):
    b, s, _ = main3.shape
    tq = min(tq, s)
    npair = FOX_W // LANES
    return pl.pallas_call(
        functools.partial(_fox_kernel, tq=tq),
        grid=(b, npair),
        in_specs=[
            pl.BlockSpec((None, s, LANES), lambda i, p: (i, 0, MAIN_FQ // LANES + p)),
            pl.BlockSpec((None, s, LANES), lambda i, p: (i, 0, MAIN_FK // LANES + p)),
            pl.BlockSpec((None, s, LANES), lambda i, p: (i, 0, MAIN_FV // LANES + p)),
            pl.BlockSpec((None, s, LANES), lambda i, p: (i, 0, 0)),
            pl.BlockSpec((None, FOX_HEADS, s), lambda i, p: (i, 0, 0)),
        ],
        out_specs=pl.BlockSpec((None, s, LANES), lambda i, p: (i, 0, p)),
        out_shape=jax.ShapeDtypeStruct((b, s, FOX_W), F32),
        scratch_shapes=[pltpu.VMEM((LANES, s), BF16), pltpu.VMEM((2, s, LANES), F32)],
        compiler_params=_params("parallel", "arbitrary"),
        name="fox",
    )(main3, main3, main3, ccol, crow)


def _gla_kernel(q_ref, k_ref, v_ref, g_ref, small_ref, w2_ref, gb_ref, hn_ref, o_ref, state_ref):
    L = q_ref.shape[0]

    @pl.when(pl.program_id(1) == 0)
    def _():
        state_ref[...] = jnp.zeros_like(state_ref)

    sm_hi, sm_mid, _ = _split3(small_ref[...])
    w_hi, w_mid, _ = _split3(w2_ref[...])
    z = _dot(sm_hi, w_hi) + _dot(sm_mid, w_hi) + _dot(sm_hi, w_mid) + gb_ref[...]
    la = _log_sigmoid(z) * (1.0 / GLA_TAU)
    bcum = _tri_cumsum(_lower_tri(L), la)
    b_last = bcum[L - 1:L, :]
    b_mid = bcum[L // 2 - 1:L // 2, :]

    q = q_ref[...].astype(F32) * (GLA_DK ** -0.5)
    k = k_ref[...].astype(F32)
    q_in = (q * jnp.exp(bcum)).astype(BF16)
    q_mid = q * jnp.exp(bcum - b_mid)
    k_mid = (k * jnp.exp(b_mid - bcum)).astype(BF16)
    k_st = (k * jnp.exp(b_last - bcum)).astype(BF16)
    v = v_ref[...]

    state_t = state_ref[...]
    o_inter = _dot_nt(q_in, state_t.astype(BF16))

    lane = lax.broadcasted_iota(jnp.int32, (1, GLA_KW), 1)
    r = lax.broadcasted_iota(jnp.int32, (L, L), 0)
    c = lax.broadcasted_iota(jnp.int32, (L, L), 1)
    causal = r >= c
    hn = hn_ref[...]
    outs = []
    for h in range(GLA_HEADS):
        in_head = (lane >= h * GLA_DK) & (lane < (h + 1) * GLA_DK)
        qh = jnp.where(in_head, q_mid, 0.0).astype(BF16)
        a = jnp.where(causal, _dot_nt(qh, k_mid), 0.0)
        vs = slice(h * GLA_DV, (h + 1) * GLA_DV)
        o = _dot(a.astype(BF16), v[:, vs]) + o_inter[:, vs]
        y = o * _rms_scale(o) * hn
        gate = g_ref[:, vs].astype(F32)
        outs.append(y * (gate * jax.nn.sigmoid(gate)))
    o_ref[...] = jnp.concatenate(outs, axis=1).astype(BF16)

    upd = _dot_tn(v, k_st)
    rr = lax.broadcasted_iota(jnp.int32, upd.shape, 0) // GLA_DV
    cc = lax.broadcasted_iota(jnp.int32, upd.shape, 1) // GLA_DK
    state_ref[...] = jnp.where(rr == cc, state_t * jnp.exp(b_last) + upd, 0.0)


def _gla(main3, small3, w2_pad, gbias, hnorm, chunk=128):
    b, s, _ = main3.shape
    L = min(chunk, s)
    return pl.pallas_call(
        _gla_kernel,
        grid=(b, s // L),
        in_specs=[
            pl.BlockSpec((None, L, GLA_KW), lambda i, c: (i, c, MAIN_GQ // GLA_KW)),
            pl.BlockSpec((None, L, GLA_KW), lambda i, c: (i, c, MAIN_GK // GLA_KW)),
            pl.BlockSpec((None, L, GLA_W), lambda i, c: (i, c, MAIN_GV // GLA_W)),
            pl.BlockSpec((None, L, GLA_W), lambda i, c: (i, c, MAIN_GG // GLA_W)),
            pl.BlockSpec((None, L, LANES), lambda i, c: (i, c, 0)),
            pl.BlockSpec((LANES, GLA_KW), lambda i, c: (0, 0)),
            pl.BlockSpec((1, GLA_KW), lambda i, c: (0, 0)),
            pl.BlockSpec((1, GLA_DV), lambda i, c: (0, 0)),
        ],
        out_specs=pl.BlockSpec((None, L, GLA_W), lambda i, c: (i, c, 0)),
        out_shape=jax.ShapeDtypeStruct((b, s, GLA_W), BF16),
        scratch_shapes=[pltpu.VMEM((GLA_W, GLA_KW), F32)],
        compiler_params=_params("parallel", "arbitrary"),
        name="gla",
    )(main3, main3, main3, main3, small3, w2_pad, gbias, hnorm)


def _gelu_tanh(x):
    return 0.5 * x * (1.0 + jnp.tanh(0.7978845608028654 * (x + 0.044715 * (x * x * x))))


def _lru_kernel(lg_ref, lx_ref, cw_ref, cb_ref, wab_ref, ba_ref, bi_ref, lam_ref, on_ref,
                o_ref, prev_ref, h_ref, a_ref, u_ref):
    T = lx_ref.shape[0]

    @pl.when(pl.program_id(1) == 0)
    def _():
        prev_ref[...] = jnp.zeros_like(prev_ref)
        h_ref[...] = jnp.zeros_like(h_ref)

    x = lx_ref[...].astype(F32)
    xcat = jnp.concatenate([prev_ref[...], x], axis=0)
    cw = cw_ref[...]
    xc = cb_ref[...] + cw[CONV_WIDTH - 1:CONV_WIDTH, :] * x
    for d in range(1, CONV_WIDTH):
        shifted = pltpu.roll(xcat, d, 0)[SUBLANES:, :]
        xc = xc + cw[CONV_WIDTH - 1 - d:CONV_WIDTH - d, :] * shifted
    prev_ref[...] = x[T - SUBLANES:, :]

    xcb = xc.astype(BF16)
    zr, zi = [], []
    for g in range(LRU_W // LANES):
        z = _dot(xcb[:, g * LANES:(g + 1) * LANES], wab_ref[g])
        zr.append(z[:, :LANES])
        zi.append(z[:, LANES:])
    r = jax.nn.sigmoid(jnp.concatenate(zr, axis=1) + ba_ref[...])
    ig = jax.nn.sigmoid(jnp.concatenate(zi, axis=1) + bi_ref[...])
    log_a = (LRU_C * r) * _log_sigmoid(lam_ref[...])
    a = jnp.exp(log_a)
    a_ref[...] = a
    u_ref[...] = jnp.sqrt(1.0 - a * a) * (ig * xc)

    def step(t, h):
        h = a_ref[pl.ds(t, 1), :] * h + u_ref[pl.ds(t, 1), :]
        u_ref[pl.ds(t, 1), :] = h
        return h

    h = lax.fori_loop(0, T, step, h_ref[...], unroll=8)
    h_ref[...] = h

    y = u_ref[...] * _gelu_tanh(lg_ref[...].astype(F32))
    o_ref[...] = (y * _rms_scale(y) * on_ref[...]).astype(BF16)


def _lru(main3, cw, cb, wab, ba, bi, lam, onorm, tile=256):
    b, s, _ = main3.shape
    T = min(tile, s)
    vec = pl.BlockSpec((1, LRU_W), lambda i, t: (0, 0))
    return pl.pallas_call(
        _lru_kernel,
        grid=(b, s // T),
        in_specs=[
            pl.BlockSpec((None, T, LRU_W), lambda i, t: (i, t, MAIN_LG // LRU_W)),
            pl.BlockSpec((None, T, LRU_W), lambda i, t: (i, t, MAIN_LX // LRU_W)),
            pl.BlockSpec((CONV_WIDTH, LRU_W), lambda i, t: (0, 0)),
            vec,
            pl.BlockSpec((LRU_W // LANES, LANES, 2 * LANES), lambda i, t: (0, 0, 0)),
            vec, vec, vec, vec,
        ],
        out_specs=pl.BlockSpec((None, T, LRU_W), lambda i, t: (i, t, 0)),
        out_shape=jax.ShapeDtypeStruct((b, s, LRU_W), BF16),
        scratch_shapes=[
            pltpu.VMEM((SUBLANES, LRU_W), F32),
            pltpu.VMEM((1, LRU_W), F32),
            pltpu.VMEM((T, LRU_W), F32),
            pltpu.VMEM((T, LRU_W), F32),
        ],
        compiler_params=_params("parallel", "arbitrary"),
        name="lru",
    )(main3, main3, cw, cb, wab, ba, bi, lam, onorm)


def _outproj_kernel(fox_ref, fn_ref, gla_ref, lru_ref, w_ref, x_ref, o_ref):
    f = fox_ref[...]
    fox_n = (f * _rms_scale(f) * fn_ref[...]).astype(BF16)
    mix = jnp.concatenate([fox_n, gla_ref[...], lru_ref[...]], axis=1)
    o_ref[...] = x_ref[...] + _dot(mix, w_ref[...])


def _outproj(fox2, fnorm, gla2, lru2, w, x2, tm):
    m, d = x2.shape
    dmix = w.shape[0]
    return pl.pallas_call(
        _outproj_kernel,
        grid=(m // tm,),
        in_specs=[
            pl.BlockSpec((tm, FOX_W), lambda i: (i, 0)),
            pl.BlockSpec((1, FOX_W), lambda i: (0, 0)),
            pl.BlockSpec((tm, GLA_W), lambda i: (i, 0)),
            pl.BlockSpec((tm, LRU_W), lambda i: (i, 0)),
            pl.BlockSpec((dmix, d), lambda i: (0, 0)),
            pl.BlockSpec((tm, d), lambda i: (i, 0)),
        ],
        out_specs=pl.BlockSpec((tm, d), lambda i: (i, 0)),
        out_shape=jax.ShapeDtypeStruct((m, d), F32),
        compiler_params=_params("parallel"),
        name="out_proj",
    )(fox2, fnorm, gla2, lru2, w, x2)


def _ffn_kernel(x_ref, g_ref, wg_ref, wu_ref, wd_ref, o_ref, xn_ref):
    @pl.when(pl.program_id(1) == 0)
    def _():
        x = x_ref[...]
        xn_ref[...] = (x * _rms_scale(x) * g_ref[...]).astype(BF16)
        o_ref[...] = x

    xn = xn_ref[...]
    gate = _dot(xn, wg_ref[...])
    up = _dot(xn, wu_ref[...])
    hid = (gate * jax.nn.sigmoid(gate) * up).astype(BF16)
    o_ref[...] += _dot(hid, wd_ref[...])


def _ffn(x2, g, wg, wu, wd, tm, th):
    m, d = x2.shape
    hdim = wg.shape[1]
    return pl.pallas_call(
        _ffn_kernel,
        grid=(m // tm, hdim // th),
        in_specs=[
            pl.BlockSpec((tm, d), lambda i, j: (i, 0)),
            pl.BlockSpec((1, d), lambda i, j: (0, 0)),
            pl.BlockSpec((d, th), lambda i, j: (0, j)),
            pl.BlockSpec((d, th), lambda i, j: (0, j)),
            pl.BlockSpec((th, d), lambda i, j: (j, 0)),
        ],
        out_specs=pl.BlockSpec((tm, d), lambda i, j: (i, 0)),
        out_shape=jax.ShapeDtypeStruct((m, d), F32),
        scratch_shapes=[pltpu.VMEM((tm, d), BF16)],
        compiler_params=_params("parallel", "arbitrary"),
        name="ffn",
    )(x2, g, wg, wu, wd)


def _final_kernel(x_ref, g_ref, o_ref):
    x = x_ref[...]
    o_ref[...] = x * _rms_scale(x) * g_ref[...]


def _final_norm(x2, g, tm):
    m, d = x2.shape
    return pl.pallas_call(
        _final_kernel,
        grid=(m // tm,),
        in_specs=[pl.BlockSpec((tm, d), lambda i: (i, 0)), pl.BlockSpec((1, d), lambda i: (0, 0))],
        out_specs=pl.BlockSpec((tm, d), lambda i: (i, 0)),
        out_shape=jax.ShapeDtypeStruct((m, d), F32),
        compiler_params=_params("parallel"),
        name="final_norm",
    )(x2, g)


def _tile(n, want):
    t = min(want, n)
    while n % t:
        t //= 2
    return t


def _prep_layer_weights(w_in_l, w2_l, w_a_l, w_i_l):
    d = w_in_l.shape[0]
    sizes = [FOX_W, FOX_W, FOX_W, FOX_HEADS, GLA_KW, GLA_KW, GLA_W, GLA_W, GLA_RANK, LRU_W, LRU_W]
    offs = [0]
    for sz in sizes:
        offs.append(offs[-1] + sz)
    col = lambda i: w_in_l[:, offs[i]:offs[i + 1]]
    wm = jnp.concatenate([col(0), col(1), col(2), col(4), col(5), col(6), col(7), col(9), col(10)],
                         axis=1).astype(BF16)
    ws = jnp.concatenate(
        [col(3), col(8), jnp.zeros((d, LANES - FOX_HEADS - GLA_RANK), w_in_l.dtype)], axis=1).astype(BF16)
    w2_pad = jnp.zeros((LANES, GLA_KW), F32).at[SMALL_GR:SMALL_GR + GLA_RANK, :].set(w2_l)
    def pair_bd(w):
        w = w.reshape(LRU_W // LANES, 2, LRU_BW, LRU_BW)
        z = jnp.zeros((LRU_W // LANES, LRU_BW, LRU_BW), w.dtype)
        top = jnp.concatenate([w[:, 0], z], axis=2)
        bot = jnp.concatenate([z, w[:, 1]], axis=2)
        return jnp.concatenate([top, bot], axis=1)
    wab = jnp.concatenate([pair_bd(w_a_l), pair_bd(w_i_l)], axis=2).astype(BF16)
    return wm, ws, w2_pad, wab


def kernel(x, norm_mix, w_in, fox_f_bias, fox_out_norm, gla_gate_w2, gla_gate_bias, gla_head_norm,
           conv_w, conv_b, lru_w_a, lru_b_a, lru_w_i, lru_b_i, lru_lambda, lru_out_norm, w_out,
           norm_ffn, w_gate, w_up, w_down, final_norm):
    b, s, d = x.shape
    depth = w_in.shape[0]
    m = b * s
    x2 = x.reshape(m, d)
    tm_big = _tile(m, 1024)
    row = lambda v: v.reshape(1, -1)

    for l in range(depth):
        wm, ws, w2_pad, wab = _prep_layer_weights(w_in[l], gla_gate_w2[l], lru_w_a[l], lru_w_i[l])
        main, small = _inproj(x2, row(norm_mix[l]), wm, ws, tm_big, _tile(MAIN_W, 1024))
        main3 = main.reshape(b, s, MAIN_W)
        small3 = small.reshape(b, s, LANES)

        fbias = jnp.zeros((1, LANES), F32).at[0, :FOX_HEADS].set(fox_f_bias[l])
        ccol, crow = _foxprep(small3, fbias)
        fox = _fox(main3, ccol, crow)

        gla = _gla(main3, small3, w2_pad, row(gla_gate_bias[l]), row(gla_head_norm[l]))

        lru = _lru(main3, conv_w[l], row(conv_b[l]), wab, row(lru_b_a[l]), row(lru_b_i[l]),
                   row(lru_lambda[l]), row(lru_out_norm[l]))

        x2 = _outproj(fox.reshape(m, FOX_W), row(fox_out_norm[l]), gla.reshape(m, GLA_W),
                      lru.reshape(m, LRU_W), w_out[l].astype(BF16), x2, _tile(m, 512))

        x2 = _ffn(x2, row(norm_ffn[l]), w_gate[l].astype(BF16), w_up[l].astype(BF16),
                  w_down[l].astype(BF16), tm_big, _tile(w_gate.shape[2], 512))

    return _final_norm(x2, row(final_norm), _tile(m, 512)).reshape(b, s, d)
```

```python
import functools

import jax
import jax.numpy as jnp
from jax import lax
from jax.experimental import pallas as pl
from jax.experimental.pallas import tpu as pltpu

F32 = jnp.float32
BF16 = jnp.bfloat16

FOX_HEADS = 8
FOX_HD = 64
FOX_W = FOX_HEADS * FOX_HD
GLA_HEADS = 4
GLA_DK = 64
GLA_DV = 128
GLA_KW = GLA_HEADS * GLA_DK
GLA_W = GLA_HEADS * GLA_DV
GLA_RANK = 16
GLA_TAU = 16.0
LRU_BLOCKS = 16
LRU_BW = 64
LRU_W = LRU_BLOCKS * LRU_BW
LRU_C = 8.0
CONV_WIDTH = 4
RMS_EPS = 1e-6

LANES = 128
SUBLANES = 8
VMEM_LIMIT = 56 * 1024 * 1024

MAIN_FQ = 0
MAIN_FK = FOX_W
MAIN_FV = 2 * FOX_W
MAIN_GQ = 3 * FOX_W
MAIN_GK = MAIN_GQ + GLA_KW
MAIN_GV = MAIN_GK + GLA_KW
MAIN_GG = MAIN_GV + GLA_W
MAIN_LG = MAIN_GG + GLA_W
MAIN_LX = MAIN_LG + LRU_W
MAIN_W = MAIN_LX + LRU_W
SMALL_FF = 0
SMALL_GR = FOX_HEADS


def _dot(a, b):
    return jnp.dot(a, b, preferred_element_type=F32)


def _dot_nt(a, b):
    return lax.dot_general(a, b, (((1,), (1,)), ((), ())), preferred_element_type=F32)


def _dot_tn(a, b):
    return lax.dot_general(a, b, (((0,), (0,)), ((), ())), preferred_element_type=F32)


def _split3(x):
    hi = x.astype(BF16)
    r1 = x - hi.astype(F32)
    mid = r1.astype(BF16)
    lo = (r1 - mid.astype(F32)).astype(BF16)
    return hi, mid, lo


def _tri_cumsum(tri, x):
    hi, mid, lo = _split3(x)
    return _dot(tri, hi) + _dot(tri, mid) + _dot(tri, lo)


def _lower_tri(n):
    r = lax.broadcasted_iota(jnp.int32, (n, n), 0)
    c = lax.broadcasted_iota(jnp.int32, (n, n), 1)
    return jnp.where(r >= c, 1.0, 0.0).astype(BF16)


def _log_sigmoid(x):
    return jnp.minimum(x, 0.0) - jnp.log1p(jnp.exp(-jnp.abs(x)))


def _rms_scale(x):
    return lax.rsqrt(jnp.mean(x * x, axis=-1, keepdims=True) + RMS_EPS)


def _params(*sem):
    return pltpu.CompilerParams(dimension_semantics=sem, vmem_limit_bytes=VMEM_LIMIT)


def _inproj_kernel(x_ref, g_ref, wm_ref, ws_ref, main_ref, small_ref, xn_ref):
    @pl.when(pl.program_id(1) == 0)
    def _():
        x = x_ref[...]
        xn = (x * _rms_scale(x) * g_ref[...]).astype(BF16)
        xn_ref[...] = xn
        small_ref[...] = _dot(xn, ws_ref[...])

    main_ref[...] = _dot(xn_ref[...], wm_ref[...]).astype(BF16)


def _inproj(x2, g, wm, ws, layer, tm, tn):
    m, d = x2.shape
    n = wm.shape[2]
    return pl.pallas_call(
        _inproj_kernel,
        grid=(m // tm, n // tn),
        in_specs=[
            pl.BlockSpec((tm, d), lambda i, j: (i, 0)),
            pl.BlockSpec((1, d), lambda i, j: (0, 0)),
            pl.BlockSpec((None, d, tn), lambda i, j: (layer, 0, j)),
            pl.BlockSpec((None, d, LANES), lambda i, j: (layer, 0, 0)),
        ],
        out_specs=[
            pl.BlockSpec((tm, tn), lambda i, j: (i, j)),
            pl.BlockSpec((tm, LANES), lambda i, j: (i, 0)),
        ],
        out_shape=[
            jax.ShapeDtypeStruct((m, n), BF16),
            jax.ShapeDtypeStruct((m, LANES), F32),
        ],
        scratch_shapes=[pltpu.VMEM((tm, d), BF16)],
        compiler_params=_params("parallel", "arbitrary"),
        name="in_proj",
    )(x2, g, wm, ws)


def _foxprep_kernel(small_ref, bias_ref, ccol_ref, crow_ref, *, blk):
    s = small_ref.shape[0]
    tri = _lower_tri(blk)
    carry = jnp.zeros((1, LANES), F32)
    for i in range(s // blk):
        x = small_ref[i * blk:(i + 1) * blk, :] + bias_ref[...]
        c = _tri_cumsum(tri, _log_sigmoid(x)) + carry
        carry = c[blk - 1:blk, :]
        ccol_ref[i * blk:(i + 1) * blk, :] = c
        crow_ref[:, i * blk:(i + 1) * blk] = c.T[0:FOX_HEADS, :]


def _foxprep(small3, bias_pad):
    b, s, _ = small3.shape
    blk = min(256, s)
    return pl.pallas_call(
        functools.partial(_foxprep_kernel, blk=blk),
        grid=(b,),
        in_specs=[
            pl.BlockSpec((None, s, LANES), lambda i: (i, 0, 0)),
            pl.BlockSpec((1, LANES), lambda i: (0, 0)),
        ],
        out_specs=[
            pl.BlockSpec((None, s, LANES), lambda i: (i, 0, 0)),
            pl.BlockSpec((None, FOX_HEADS, s), lambda i: (i, 0, 0)),
        ],
        out_shape=[
            jax.ShapeDtypeStruct((b, s, LANES), F32),
            jax.ShapeDtypeStruct((b, FOX_HEADS, s), F32),
        ],
        compiler_params=_params("parallel"),
        name="fox_prep",
    )(small3, bias_pad)


def _fox_kernel(q_ref, k_ref, v_ref, ccol_ref, crow_ref, o_ref, vt_ref, csb_ref, *, tq):
    s = q_ref.shape[0]
    pair = pl.program_id(1)
    lane = lax.broadcasted_iota(jnp.int32, (1, LANES), 1)
    first = lane < FOX_HD

    vt_ref[...] = v_ref[...].astype(F32).T.astype(BF16)
    cc = ccol_ref[...]
    for hh in range(2):
        col = jnp.sum(jnp.where(lane == 2 * pair + hh, cc, 0.0), axis=1, keepdims=True)
        csb_ref[hh] = jnp.broadcast_to(col, (s, LANES))

    kpos = lax.broadcasted_iota(jnp.int32, (tq, tq), 0)
    qpos = lax.broadcasted_iota(jnp.int32, (tq, tq), 1)
    causal = kpos <= qpos
    top_half = lax.broadcasted_iota(jnp.int32, (LANES, 1), 0) < FOX_HD

    for i in range(s // tq):
        q0, p_len = i * tq, (i + 1) * tq
        q = q_ref[q0:p_len, :] * (FOX_HD ** -0.5)
        o_heads = []
        for hh in range(2):
            sel = first if hh == 0 else jnp.logical_not(first)
            qm = jnp.where(sel, q, jnp.zeros_like(q))
            ct = crow_ref[pl.ds(2 * pair + hh, 1), q0:p_len]
            cs = csb_ref[hh, 0:p_len, :]
            st = (_dot_nt(k_ref[0:p_len, :], qm) + ct) - jnp.concatenate([cs] * (tq // LANES), axis=1)
            diag = jnp.where(causal, st[q0:p_len, :], -jnp.inf)
            st = diag if i == 0 else jnp.concatenate([st[0:q0, :], diag], axis=0)
            m = jnp.max(st, axis=0, keepdims=True)
            p = jnp.exp(st - m)
            l = jnp.sum(p, axis=0, keepdims=True)
            ot = _dot(vt_ref[:, 0:p_len], p.astype(BF16))
            o_heads.append(ot * (1.0 / l))
        o_ref[q0:p_len, :] = jnp.where(top_half, o_heads[0], o_heads[1]).T


def _fox(main3, ccol, crow, tq=256):
    b, s, _ = main3.shape
    tq = min(tq, s)
    npair = FOX_W // LANES
    return pl.pallas_call(
        functools.partial(_fox_kernel, tq=tq),
        grid=(b, npair),
        in_specs=[
            pl.BlockSpec((None, s, LANES), lambda i, p: (i, 0, MAIN_FQ // LANES + p)),
            pl.BlockSpec((None, s, LANES), lambda i, p: (i, 0, MAIN_FK // LANES + p)),
            pl.BlockSpec((None, s, LANES), lambda i, p: (i, 0, MAIN_FV // LANES + p)),
            pl.BlockSpec((None, s, LANES), lambda i, p: (i, 0, 0)),
            pl.BlockSpec((None, FOX_HEADS, s), lambda i, p: (i, 0, 0)),
        ],
        out_specs=pl.BlockSpec((None, s, LANES), lambda i, p: (i, 0, p)),
        out_shape=jax.ShapeDtypeStruct((b, s, FOX_W), F32),
        scratch_shapes=[pltpu.VMEM((LANES, s), BF16), pltpu.VMEM((2, s, LANES), F32)],
        compiler_params=_params("parallel", "arbitrary"),
        name="fox",
    )(main3, main3, main3, ccol, crow)


def _gla_kernel(q_ref, k_ref, v_ref, g_ref, small_ref, w2_ref, gb_ref, hn_ref, o_ref, state_ref):
    @pl.when(pl.program_id(0) == 0)
    def _():
        state_ref[...] = jnp.zeros_like(state_ref)

    for bi in range(q_ref.shape[0]):
        _gla_chunk(q_ref.at[bi], k_ref.at[bi], v_ref.at[bi], g_ref.at[bi], small_ref.at[bi],
                   w2_ref, gb_ref, hn_ref, o_ref.at[bi], state_ref.at[bi])


def _gla_chunk(q_ref, k_ref, v_ref, g_ref, small_ref, w2_ref, gb_ref, hn_ref, o_ref, state_ref):
    L = q_ref.shape[0]

    sm_hi, sm_mid, _ = _split3(small_ref[...])
    w_hi, w_mid, _ = _split3(w2_ref[...])
    z = _dot(sm_hi, w_hi) + _dot(sm_mid, w_hi) + _dot(sm_hi, w_mid) + gb_ref[...]
    la = _log_sigmoid(z) * (1.0 / GLA_TAU)
    bcum = _tri_cumsum(_lower_tri(L), la)
    b_last = bcum[L - 1:L, :]
    b_mid = bcum[L // 2 - 1:L // 2, :]

    q = q_ref[...].astype(F32) * (GLA_DK ** -0.5)
    k = k_ref[...].astype(F32)
    q_in = (q * jnp.exp(bcum)).astype(BF16)
    q_mid = q * jnp.exp(bcum - b_mid)
    k_mid = (k * jnp.exp(b_mid - bcum)).astype(BF16)
    k_st = (k * jnp.exp(b_last - bcum)).astype(BF16)
    v = v_ref[...]

    state_t = state_ref[...]
    o_inter = _dot_nt(q_in, state_t.astype(BF16))

    lane = lax.broadcasted_iota(jnp.int32, (1, GLA_KW), 1)
    r = lax.broadcasted_iota(jnp.int32, (L, L), 0)
    c = lax.broadcasted_iota(jnp.int32, (L, L), 1)
    causal = r >= c
    hn = hn_ref[...]
    outs = []
    for h in range(GLA_HEADS):
        in_head = (lane >= h * GLA_DK) & (lane < (h + 1) * GLA_DK)
        qh = jnp.where(in_head, q_mid, 0.0).astype(BF16)
        a = jnp.where(causal, _dot_nt(qh, k_mid), 0.0)
        vs = slice(h * GLA_DV, (h + 1) * GLA_DV)
        o = _dot(a.astype(BF16), v[:, vs]) + o_inter[:, vs]
        y = o * _rms_scale(o) * hn
        gate = g_ref[:, vs].astype(F32)
        outs.append(y * (gate * jax.nn.sigmoid(gate)))
    o_ref[...] = jnp.concatenate(outs, axis=1).astype(BF16)

    upd = _dot_tn(v, k_st)
    rr = lax.broadcasted_iota(jnp.int32, upd.shape, 0) // GLA_DV
    cc = lax.broadcasted_iota(jnp.int32, upd.shape, 1) // GLA_DK
    state_ref[...] = jnp.where(rr == cc, state_t * jnp.exp(b_last) + upd, 0.0)


def _gla(main3, small3, w2_pad, gbias, hnorm, chunk=128):
    b, s, _ = main3.shape
    L = min(chunk, s)
    return pl.pallas_call(
        _gla_kernel,
        grid=(s // L,),
        in_specs=[
            pl.BlockSpec((b, L, GLA_KW), lambda c: (0, c, MAIN_GQ // GLA_KW)),
            pl.BlockSpec((b, L, GLA_KW), lambda c: (0, c, MAIN_GK // GLA_KW)),
            pl.BlockSpec((b, L, GLA_W), lambda c: (0, c, MAIN_GV // GLA_W)),
            pl.BlockSpec((b, L, GLA_W), lambda c: (0, c, MAIN_GG // GLA_W)),
            pl.BlockSpec((b, L, LANES), lambda c: (0, c, 0)),
            pl.BlockSpec((LANES, GLA_KW), lambda c: (0, 0)),
            pl.BlockSpec((1, GLA_KW), lambda c: (0, 0)),
            pl.BlockSpec((1, GLA_DV), lambda c: (0, 0)),
        ],
        out_specs=pl.BlockSpec((b, L, GLA_W), lambda c: (0, c, 0)),
        out_shape=jax.ShapeDtypeStruct((b, s, GLA_W), BF16),
        scratch_shapes=[pltpu.VMEM((b, GLA_W, GLA_KW), F32)],
        compiler_params=_params("arbitrary"),
        name="gla",
    )(main3, main3, main3, main3, small3, w2_pad, gbias, hnorm)


def _gelu_tanh(x):
    return 0.5 * x * (1.0 + jnp.tanh(0.7978845608028654 * (x + 0.044715 * (x * x * x))))


def _lru_kernel(lg_ref, lx_ref, cw_ref, cb_ref, wab_ref, ba_ref, bi_ref, lam_ref, on_ref,
                o_ref, prev_ref, h_ref, a_ref, u_ref):
    T = lx_ref.shape[0]

    @pl.when(pl.program_id(1) == 0)
    def _():
        prev_ref[...] = jnp.zeros_like(prev_ref)
        h_ref[...] = jnp.zeros_like(h_ref)

    x = lx_ref[...].astype(F32)
    xcat = jnp.concatenate([prev_ref[...], x], axis=0)
    cw = cw_ref[...]
    xc = cb_ref[...] + cw[CONV_WIDTH - 1:CONV_WIDTH, :] * x
    for d in range(1, CONV_WIDTH):
        shifted = pltpu.roll(xcat, d, 0)[SUBLANES:, :]
        xc = xc + cw[CONV_WIDTH - 1 - d:CONV_WIDTH - d, :] * shifted
    prev_ref[...] = x[T - SUBLANES:, :]

    xcb = xc.astype(BF16)
    zr, zi = [], []
    for g in range(LRU_W // LANES):
        z = _dot(xcb[:, g * LANES:(g + 1) * LANES], wab_ref[g])
        zr.append(z[:, :LANES])
        zi.append(z[:, LANES:])
    r = jax.nn.sigmoid(jnp.concatenate(zr, axis=1) + ba_ref[...])
    ig = jax.nn.sigmoid(jnp.concatenate(zi, axis=1) + bi_ref[...])
    log_a = (LRU_C * r) * _log_sigmoid(lam_ref[...])
    a = jnp.exp(log_a)
    a_ref[...] = a
    u_ref[...] = jnp.sqrt(1.0 - a * a) * (ig * xc)

    def step(t, h):
        h = a_ref[pl.ds(t, 1), :] * h + u_ref[pl.ds(t, 1), :]
        u_ref[pl.ds(t, 1), :] = h
        return h

    h = lax.fori_loop(0, T, step, h_ref[...], unroll=8)
    h_ref[...] = h

    y = u_ref[...] * _gelu_tanh(lg_ref[...].astype(F32))
    o_ref[...] = (y * _rms_scale(y) * on_ref[...]).astype(BF16)


def _lru(main3, cw, cb, wab, ba, bi, lam, onorm, tile=256):
    b, s, _ = main3.shape
    T = min(tile, s)
    vec = pl.BlockSpec((1, LRU_W), lambda i, t: (0, 0))
    return pl.pallas_call(
        _lru_kernel,
        grid=(b, s // T),
        in_specs=[
            pl.BlockSpec((None, T, LRU_W), lambda i, t: (i, t, MAIN_LG // LRU_W)),
            pl.BlockSpec((None, T, LRU_W), lambda i, t: (i, t, MAIN_LX // LRU_W)),
            pl.BlockSpec((CONV_WIDTH, LRU_W), lambda i, t: (0, 0)),
            vec,
            pl.BlockSpec((LRU_W // LANES, LANES, 2 * LANES), lambda i, t: (0, 0, 0)),
            vec, vec, vec, vec,
        ],
        out_specs=pl.BlockSpec((None, T, LRU_W), lambda i, t: (i, t, 0)),
        out_shape=jax.ShapeDtypeStruct((b, s, LRU_W), BF16),
        scratch_shapes=[
            pltpu.VMEM((SUBLANES, LRU_W), F32),
            pltpu.VMEM((1, LRU_W), F32),
            pltpu.VMEM((T, LRU_W), F32),
            pltpu.VMEM((T, LRU_W), F32),
        ],
        compiler_params=_params("parallel", "arbitrary"),
        name="lru",
    )(main3, main3, cw, cb, wab, ba, bi, lam, onorm)


def _outproj_kernel(fox_ref, fn_ref, gla_ref, lru_ref, w_ref, x_ref, o_ref):
    f = fox_ref[...]
    fox_n = (f * _rms_scale(f) * fn_ref[...]).astype(BF16)
    mix = jnp.concatenate([fox_n, gla_ref[...], lru_ref[...]], axis=1)
    o_ref[...] = x_ref[...] + _dot(mix, w_ref[...])


def _outproj(fox2, fnorm, gla2, lru2, w, layer, x2, tm):
    m, d = x2.shape
    dmix = w.shape[1]
    return pl.pallas_call(
        _outproj_kernel,
        grid=(m // tm,),
        in_specs=[
            pl.BlockSpec((tm, FOX_W), lambda i: (i, 0)),
            pl.BlockSpec((1, FOX_W), lambda i: (0, 0)),
            pl.BlockSpec((tm, GLA_W), lambda i: (i, 0)),
            pl.BlockSpec((tm, LRU_W), lambda i: (i, 0)),
            pl.BlockSpec((None, dmix, d), lambda i: (layer, 0, 0)),
            pl.BlockSpec((tm, d), lambda i: (i, 0)),
        ],
        out_specs=pl.BlockSpec((tm, d), lambda i: (i, 0)),
        out_shape=jax.ShapeDtypeStruct((m, d), F32),
        compiler_params=_params("parallel"),
        name="out_proj",
    )(fox2, fnorm, gla2, lru2, w, x2)


def _ffn_kernel(x_ref, g_ref, wg_ref, wu_ref, wd_ref, o_ref, xn_ref):
    @pl.when(pl.program_id(1) == 0)
    def _():
        x = x_ref[...]
        xn_ref[...] = (x * _rms_scale(x) * g_ref[...]).astype(BF16)
        o_ref[...] = x

    xn = xn_ref[...]
    gate = _dot(xn, wg_ref[...])
    up = _dot(xn, wu_ref[...])
    hid = (gate * jax.nn.sigmoid(gate) * up).astype(BF16)
    o_ref[...] += _dot(hid, wd_ref[...])


def _ffn(x2, g, wg, wu, wd, layer, tm, th):
    m, d = x2.shape
    hdim = wg.shape[2]
    return pl.pallas_call(
        _ffn_kernel,
        grid=(m // tm, hdim // th),
        in_specs=[
            pl.BlockSpec((tm, d), lambda i, j: (i, 0)),
            pl.BlockSpec((1, d), lambda i, j: (0, 0)),
            pl.BlockSpec((None, d, th), lambda i, j: (layer, 0, j)),
            pl.BlockSpec((None, d, th), lambda i, j: (layer, 0, j)),
            pl.BlockSpec((None, th, d), lambda i, j: (layer, j, 0)),
        ],
        out_specs=pl.BlockSpec((tm, d), lambda i, j: (i, 0)),
        out_shape=jax.ShapeDtypeStruct((m, d), F32),
        scratch_shapes=[pltpu.VMEM((tm, d), BF16)],
        compiler_params=_params("parallel", "arbitrary"),
        name="ffn",
    )(x2, g, wg, wu, wd)


def _final_kernel(x_ref, g_ref, o_ref):
    x = x_ref[...]
    o_ref[...] = x * _rms_scale(x) * g_ref[...]


def _final_norm(x2, g, tm):
    m, d = x2.shape
    return pl.pallas_call(
        _final_kernel,
        grid=(m // tm,),
        in_specs=[pl.BlockSpec((tm, d), lambda i: (i, 0)), pl.BlockSpec((1, d), lambda i: (0, 0))],
        out_specs=pl.BlockSpec((tm, d), lambda i: (i, 0)),
        out_shape=jax.ShapeDtypeStruct((m, d), F32),
        compiler_params=_params("parallel"),
        name="final_norm",
    )(x2, g)


def _tile(n, want):
    t = min(want, n)
    while n % t:
        t //= 2
    return t


def _prep_weights(w_in, w2, w_a, w_i):
    depth, d, _ = w_in.shape
    sizes = [FOX_W, FOX_W, FOX_W, FOX_HEADS, GLA_KW, GLA_KW, GLA_W, GLA_W, GLA_RANK, LRU_W, LRU_W]
    offs = [0]
    for sz in sizes:
        offs.append(offs[-1] + sz)
    col = lambda i: w_in[:, :, offs[i]:offs[i + 1]]
    wm = jnp.concatenate([col(0), col(1), col(2), col(4), col(5), col(6), col(7), col(9), col(10)],
                         axis=2).astype(BF16)
    ws = jnp.concatenate(
        [col(3), col(8), jnp.zeros((depth, d, LANES - FOX_HEADS - GLA_RANK), w_in.dtype)],
        axis=2).astype(BF16)
    w2_pad = jnp.zeros((depth, LANES, GLA_KW), F32).at[:, SMALL_GR:SMALL_GR + GLA_RANK, :].set(w2)
    def pair_bd(w):
        w = w.reshape(depth, LRU_W // LANES, 2, LRU_BW, LRU_BW)
        z = jnp.zeros((depth, LRU_W // LANES, LRU_BW, LRU_BW), w.dtype)
        top = jnp.concatenate([w[:, :, 0], z], axis=3)
        bot = jnp.concatenate([z, w[:, :, 1]], axis=3)
        return jnp.concatenate([top, bot], axis=2)
    wab = jnp.concatenate([pair_bd(w_a), pair_bd(w_i)], axis=3).astype(BF16)
    return wm, ws, w2_pad, wab


def kernel(x, norm_mix, w_in, fox_f_bias, fox_out_norm, gla_gate_w2, gla_gate_bias, gla_head_norm,
           conv_w, conv_b, lru_w_a, lru_b_a, lru_w_i, lru_b_i, lru_lambda, lru_out_norm, w_out,
           norm_ffn, w_gate, w_up, w_down, final_norm):
    b, s, d = x.shape
    depth = w_in.shape[0]
    m = b * s
    x2 = x.reshape(m, d)
    tm_big = _tile(m, 1024)
    row = lambda v: v.reshape(1, -1)

    wm, ws, w2_pad, wab = _prep_weights(w_in, gla_gate_w2, lru_w_a, lru_w_i)
    w_out16, w_gate16, w_up16, w_down16 = (w.astype(BF16) for w in (w_out, w_gate, w_up, w_down))
    fbias = jnp.zeros((depth, 1, LANES), F32).at[:, 0, :FOX_HEADS].set(fox_f_bias)

    for l in range(depth):
        main, small = _inproj(x2, row(norm_mix[l]), wm, ws, l, tm_big, _tile(MAIN_W, 1024))
        main3 = main.reshape(b, s, MAIN_W)
        small3 = small.reshape(b, s, LANES)

        ccol, crow = _foxprep(small3, fbias[l])
        fox = _fox(main3, ccol, crow)

        gla = _gla(main3, small3, w2_pad[l], row(gla_gate_bias[l]), row(gla_head_norm[l]))

        lru = _lru(main3, conv_w[l], row(conv_b[l]), wab[l], row(lru_b_a[l]), row(lru_b_i[l]),
                   row(lru_lambda[l]), row(lru_out_norm[l]))

        x2 = _outproj(fox.reshape(m, FOX_W), row(fox_out_norm[l]), gla.reshape(m, GLA_W),
                      lru.reshape(m, LRU_W), w_out16, l, x2, _tile(m, 512))

        x2 = _ffn(x2, row(norm_ffn[l]), w_gate16, w_up16, w_down16, l, tm_big,
                  _tile(w_gate.shape[2], 512))

    return _final_norm(x2, row(final_norm), _tile(m, 512)).reshape(b, s, d)
```

```python
import functools

import jax
import jax.numpy as jnp
from jax import lax
from jax.experimental import pallas as pl
from jax.experimental.pallas import tpu as pltpu

F32 = jnp.float32
BF16 = jnp.bfloat16

FOX_HEADS = 8
FOX_HD = 64
FOX_W = FOX_HEADS * FOX_HD
GLA_HEADS = 4
GLA_DK = 64
GLA_DV = 128
GLA_KW = GLA_HEADS * GLA_DK
GLA_W = GLA_HEADS * GLA_DV
GLA_RANK = 16
GLA_TAU = 16.0
LRU_BLOCKS = 16
LRU_BW = 64
LRU_W = LRU_BLOCKS * LRU_BW
LRU_C = 8.0
CONV_WIDTH = 4
RMS_EPS = 1e-6

LANES = 128
SUBLANES = 8
VMEM_LIMIT = 56 * 1024 * 1024

MAIN_FQ = 0
MAIN_FK = FOX_W
MAIN_FV = 2 * FOX_W
MAIN_GQ = 3 * FOX_W
MAIN_GK = MAIN_GQ + GLA_KW
MAIN_GV = MAIN_GK + GLA_KW
MAIN_GG = MAIN_GV + GLA_W
MAIN_LG = MAIN_GG + GLA_W
MAIN_LX = MAIN_LG + LRU_W
MAIN_W = MAIN_LX + LRU_W
SMALL_FF = 0
SMALL_GR = FOX_HEADS


def _dot(a, b):
    return jnp.dot(a, b, preferred_element_type=F32)


def _dot_nt(a, b):
    return lax.dot_general(a, b, (((1,), (1,)), ((), ())), preferred_element_type=F32)


def _dot_tn(a, b):
    return lax.dot_general(a, b, (((0,), (0,)), ((), ())), preferred_element_type=F32)


def _split3(x):
    hi = x.astype(BF16)
    r1 = x - hi.astype(F32)
    mid = r1.astype(BF16)
    lo = (r1 - mid.astype(F32)).astype(BF16)
    return hi, mid, lo


def _tri_cumsum(tri, x):
    hi, mid, lo = _split3(x)
    return _dot(tri, hi) + _dot(tri, mid) + _dot(tri, lo)


def _lower_tri(n):
    r = lax.broadcasted_iota(jnp.int32, (n, n), 0)
    c = lax.broadcasted_iota(jnp.int32, (n, n), 1)
    return jnp.where(r >= c, 1.0, 0.0).astype(BF16)


def _log_sigmoid(x):
    return jnp.minimum(x, 0.0) - jnp.log1p(jnp.exp(-jnp.abs(x)))


def _rms_scale(x):
    return lax.rsqrt(jnp.mean(x * x, axis=-1, keepdims=True) + RMS_EPS)


def _params(*sem):
    return pltpu.CompilerParams(dimension_semantics=sem, vmem_limit_bytes=VMEM_LIMIT)


def _inproj_kernel(x_ref, g_ref, wm_ref, ws_ref, main_ref, small_ref, xn_ref):
    @pl.when(pl.program_id(1) == 0)
    def _():
        x = x_ref[...]
        xn = (x * _rms_scale(x) * g_ref[...]).astype(BF16)
        xn_ref[...] = xn
        small_ref[...] = _dot(xn, ws_ref[...])

    main_ref[...] = _dot(xn_ref[...], wm_ref[...]).astype(BF16)


def _inproj(x2, g, wm, ws, layer, tm, tn):
    m, d = x2.shape
    n = wm.shape[2]
    return pl.pallas_call(
        _inproj_kernel,
        grid=(m // tm, n // tn),
        in_specs=[
            pl.BlockSpec((tm, d), lambda i, j: (i, 0)),
            pl.BlockSpec((1, d), lambda i, j: (0, 0)),
            pl.BlockSpec((None, d, tn), lambda i, j: (layer, 0, j)),
            pl.BlockSpec((None, d, LANES), lambda i, j: (layer, 0, 0)),
        ],
        out_specs=[
            pl.BlockSpec((tm, tn), lambda i, j: (i, j)),
            pl.BlockSpec((tm, LANES), lambda i, j: (i, 0)),
        ],
        out_shape=[
            jax.ShapeDtypeStruct((m, n), BF16),
            jax.ShapeDtypeStruct((m, LANES), F32),
        ],
        scratch_shapes=[pltpu.VMEM((tm, d), BF16)],
        compiler_params=_params("parallel", "arbitrary"),
        name="in_proj",
    )(x2, g, wm, ws)


def _foxprep_kernel(small_ref, bias_ref, ccol_ref, *, blk):
    s = small_ref.shape[0]
    tri = _lower_tri(blk)
    carry = jnp.zeros((1, LANES), F32)
    for i in range(s // blk):
        x = small_ref[i * blk:(i + 1) * blk, :] + bias_ref[...]
        c = _tri_cumsum(tri, _log_sigmoid(x)) + carry
        carry = c[blk - 1:blk, :]
        ccol_ref[i * blk:(i + 1) * blk, :] = c


def _foxprep(small3, bias_pad):
    b, s, _ = small3.shape
    blk = min(256, s)
    return pl.pallas_call(
        functools.partial(_foxprep_kernel, blk=blk),
        grid=(b,),
        in_specs=[
            pl.BlockSpec((None, s, LANES), lambda i: (i, 0, 0)),
            pl.BlockSpec((1, LANES), lambda i: (0, 0)),
        ],
        out_specs=pl.BlockSpec((None, s, LANES), lambda i: (i, 0, 0)),
        out_shape=jax.ShapeDtypeStruct((b, s, LANES), F32),
        compiler_params=_params("parallel"),
        name="fox_prep",
    )(small3, bias_pad)


LOG2E = 1.4426950408889634


def _fox_kernel(q_ref, k_ref, v_ref, ccol_ref, o_ref, vt_ref, kx_ref, qx_ref, *, tq):
    s = q_ref.shape[0]
    pair = pl.program_id(1)
    lane = lax.broadcasted_iota(jnp.int32, (1, LANES), 1)
    first = lane < FOX_HD

    vt_ref[...] = v_ref[...].astype(F32).T.astype(BF16)
    cc = ccol_ref[...]
    for hh in range(2):
        col = jnp.sum(jnp.where(lane == 2 * pair + hh, cc, 0.0), axis=1, keepdims=True)
        hi, mid, lo = (t.astype(F32) for t in _split3(jnp.broadcast_to(col * LOG2E, (s, LANES))))
        kx = jnp.where(lane < 3, 1.0, jnp.where(lane == 3, -hi, jnp.where(lane == 4, -mid,
                       jnp.where(lane == 5, -lo, 0.0))))
        qx = jnp.where(lane == 0, hi, jnp.where(lane == 1, mid, jnp.where(lane == 2, lo,
                       jnp.where(lane < 6, 1.0, 0.0))))
        kx_ref[hh] = kx.astype(BF16)
        qx_ref[hh] = qx.astype(BF16)

    kpos = lax.broadcasted_iota(jnp.int32, (tq, tq), 0)
    qpos = lax.broadcasted_iota(jnp.int32, (tq, tq), 1)
    causal = kpos <= qpos
    top_half = lax.broadcasted_iota(jnp.int32, (LANES, 1), 0) < FOX_HD

    for i in range(s // tq):
        q0, p_len = i * tq, (i + 1) * tq
        q = (q_ref[q0:p_len, :].astype(F32) * (FOX_HD ** -0.5 * LOG2E)).astype(BF16)
        o_heads = []
        for hh in range(2):
            sel = first if hh == 0 else jnp.logical_not(first)
            qa = jnp.concatenate([jnp.where(sel, q, jnp.zeros_like(q)), qx_ref[hh, q0:p_len, :]], axis=1)
            ka = jnp.concatenate([k_ref[0:p_len, :], kx_ref[hh, 0:p_len, :]], axis=1)
            st = _dot_nt(ka, qa)
            diag = jnp.where(causal, st[q0:p_len, :], -jnp.inf)
            st = diag if i == 0 else jnp.concatenate([st[0:q0, :], diag], axis=0)
            m = jnp.max(st, axis=0, keepdims=True)
            p = jnp.exp2(st - m)
            l = jnp.sum(p, axis=0, keepdims=True)
            ot = _dot(vt_ref[:, 0:p_len], p.astype(BF16))
            o_heads.append(ot * (1.0 / l))
        o_ref[q0:p_len, :] = jnp.where(top_half, o_heads[0], o_heads[1]).T


def _fox(main3, ccol, tq=256):
    b, s, _ = main3.shape
    tq = min(tq, s)
    npair = FOX_W // LANES
    return pl.pallas_call(
        functools.partial(_fox_kernel, tq=tq),
        grid=(b, npair),
        in_specs=[
            pl.BlockSpec((None, s, LANES), lambda i, p: (i, 0, MAIN_FQ // LANES + p)),
            pl.BlockSpec((None, s, LANES), lambda i, p: (i, 0, MAIN_FK // LANES + p)),
            pl.BlockSpec((None, s, LANES), lambda i, p: (i, 0, MAIN_FV // LANES + p)),
            pl.BlockSpec((None, s, LANES), lambda i, p: (i, 0, 0)),
        ],
        out_specs=pl.BlockSpec((None, s, LANES), lambda i, p: (i, 0, p)),
        out_shape=jax.ShapeDtypeStruct((b, s, FOX_W), F32),
        scratch_shapes=[pltpu.VMEM((LANES, s), BF16), pltpu.VMEM((2, s, LANES), BF16),
                        pltpu.VMEM((2, s, LANES), BF16)],
        compiler_params=_params("parallel", "arbitrary"),
        name="fox",
    )(main3, main3, main3, ccol)


def _gla_kernel(q_ref, k_ref, v_ref, g_ref, small_ref, w2_ref, gb_ref, hn_ref, o_ref, state_ref):
    @pl.when(pl.program_id(0) == 0)
    def _():
        state_ref[...] = jnp.zeros_like(state_ref)

    for bi in range(q_ref.shape[0]):
        _gla_chunk(q_ref.at[bi], k_ref.at[bi], v_ref.at[bi], g_ref.at[bi], small_ref.at[bi],
                   w2_ref, gb_ref, hn_ref, o_ref.at[bi], state_ref.at[bi])


def _gla_chunk(q_ref, k_ref, v_ref, g_ref, small_ref, w2_ref, gb_ref, hn_ref, o_ref, state_ref):
    L = q_ref.shape[0]

    sm_hi, sm_mid, _ = _split3(small_ref[...])
    w_hi, w_mid, _ = _split3(w2_ref[...])
    z = _dot(sm_hi, w_hi) + _dot(sm_mid, w_hi) + _dot(sm_hi, w_mid) + gb_ref[...]
    la = _log_sigmoid(z) * (1.0 / GLA_TAU)
    bcum = _tri_cumsum(_lower_tri(L), la)
    b_last = bcum[L - 1:L, :]
    b_mid = bcum[L // 2 - 1:L // 2, :]

    q = q_ref[...].astype(F32) * (GLA_DK ** -0.5)
    k = k_ref[...].astype(F32)
    q_in = (q * jnp.exp(bcum)).astype(BF16)
    q_mid = q * jnp.exp(bcum - b_mid)
    k_mid = (k * jnp.exp(b_mid - bcum)).astype(BF16)
    k_st = (k * jnp.exp(b_last - bcum)).astype(BF16)
    v = v_ref[...]

    state_t = state_ref[...]
    o_inter = _dot_nt(q_in, state_t.astype(BF16))

    lane = lax.broadcasted_iota(jnp.int32, (1, GLA_KW), 1)
    r = lax.broadcasted_iota(jnp.int32, (L, L), 0)
    c = lax.broadcasted_iota(jnp.int32, (L, L), 1)
    causal = r >= c
    hn = hn_ref[...]
    outs = []
    for h in range(GLA_HEADS):
        in_head = (lane >= h * GLA_DK) & (lane < (h + 1) * GLA_DK)
        qh = jnp.where(in_head, q_mid, 0.0).astype(BF16)
        a = jnp.where(causal, _dot_nt(qh, k_mid), 0.0)
        vs = slice(h * GLA_DV, (h + 1) * GLA_DV)
        o = _dot(a.astype(BF16), v[:, vs]) + o_inter[:, vs]
        y = o * _rms_scale(o) * hn
        gate = g_ref[:, vs].astype(F32)
        outs.append(y * (gate * jax.nn.sigmoid(gate)))
    o_ref[...] = jnp.concatenate(outs, axis=1).astype(BF16)

    upd = _dot_tn(v, k_st)
    rr = lax.broadcasted_iota(jnp.int32, upd.shape, 0) // GLA_DV
    cc = lax.broadcasted_iota(jnp.int32, upd.shape, 1) // GLA_DK
    state_ref[...] = jnp.where(rr == cc, state_t * jnp.exp(b_last) + upd, 0.0)


def _gla(main3, small3, w2_pad, gbias, hnorm, chunk=128):
    b, s, _ = main3.shape
    L = min(chunk, s)
    return pl.pallas_call(
        _gla_kernel,
        grid=(s // L,),
        in_specs=[
            pl.BlockSpec((b, L, GLA_KW), lambda c: (0, c, MAIN_GQ // GLA_KW)),
            pl.BlockSpec((b, L, GLA_KW), lambda c: (0, c, MAIN_GK // GLA_KW)),
            pl.BlockSpec((b, L, GLA_W), lambda c: (0, c, MAIN_GV // GLA_W)),
            pl.BlockSpec((b, L, GLA_W), lambda c: (0, c, MAIN_GG // GLA_W)),
            pl.BlockSpec((b, L, LANES), lambda c: (0, c, 0)),
            pl.BlockSpec((LANES, GLA_KW), lambda c: (0, 0)),
            pl.BlockSpec((1, GLA_KW), lambda c: (0, 0)),
            pl.BlockSpec((1, GLA_DV), lambda c: (0, 0)),
        ],
        out_specs=pl.BlockSpec((b, L, GLA_W), lambda c: (0, c, 0)),
        out_shape=jax.ShapeDtypeStruct((b, s, GLA_W), BF16),
        scratch_shapes=[pltpu.VMEM((b, GLA_W, GLA_KW), F32)],
        compiler_params=_params("arbitrary"),
        name="gla",
    )(main3, main3, main3, main3, small3, w2_pad, gbias, hnorm)


def _gelu_tanh(x):
    return 0.5 * x * (1.0 + jnp.tanh(0.7978845608028654 * (x + 0.044715 * (x * x * x))))


def _lru_kernel(lg_ref, lx_ref, cw_ref, cb_ref, wab_ref, ba_ref, bi_ref, lam_ref, on_ref,
                o_ref, prev_ref, h_ref, a_ref, u_ref):
    nb, T, _ = lx_ref.shape
    rows = LRU_W // LANES

    @pl.when(pl.program_id(0) == 0)
    def _():
        prev_ref[...] = jnp.zeros_like(prev_ref)
        h_ref[...] = jnp.zeros_like(h_ref)

    cw = cw_ref[...]
    lam_ls = _log_sigmoid(lam_ref[...])
    for bi in range(nb):
        x = lx_ref[bi].astype(F32)
        xcat = jnp.concatenate([prev_ref[bi], x], axis=0)
        xc = cb_ref[...] + cw[CONV_WIDTH - 1:CONV_WIDTH, :] * x
        for d in range(1, CONV_WIDTH):
            shifted = pltpu.roll(xcat, d, 0)[SUBLANES:, :]
            xc = xc + cw[CONV_WIDTH - 1 - d:CONV_WIDTH - d, :] * shifted
        prev_ref[bi] = x[T - SUBLANES:, :]

        xcb = xc.astype(BF16)
        zr, zi = [], []
        for g in range(rows):
            z = _dot(xcb[:, g * LANES:(g + 1) * LANES], wab_ref[g])
            zr.append(z[:, :LANES])
            zi.append(z[:, LANES:])
        r = jax.nn.sigmoid(jnp.concatenate(zr, axis=1) + ba_ref[...])
        ig = jax.nn.sigmoid(jnp.concatenate(zi, axis=1) + bi_ref[...])
        a = jnp.exp((LRU_C * r) * lam_ls)
        u = jnp.sqrt(1.0 - a * a) * (ig * xc)
        a_ref[bi] = a.reshape(T, rows, LANES)
        u_ref[bi] = u.reshape(T, rows, LANES)

    def step(t, hs):
        out = []
        for bi in range(nb):
            h = a_ref[bi, t] * hs[bi] + u_ref[bi, t]
            u_ref[bi, t] = h
            out.append(h)
        return tuple(out)

    hs = lax.fori_loop(0, T, step, tuple(h_ref[bi] for bi in range(nb)), unroll=8)
    for bi in range(nb):
        h_ref[bi] = hs[bi]

    for bi in range(nb):
        y = u_ref[bi].reshape(T, LRU_W) * _gelu_tanh(lg_ref[bi].astype(F32))
        o_ref[bi] = (y * _rms_scale(y) * on_ref[...]).astype(BF16)


def _lru(main3, cw, cb, wab, ba, bi, lam, onorm, tile=256):
    b, s, _ = main3.shape
    T = min(tile, s)
    rows = LRU_W // LANES
    vec = pl.BlockSpec((1, LRU_W), lambda t: (0, 0))
    return pl.pallas_call(
        _lru_kernel,
        grid=(s // T,),
        in_specs=[
            pl.BlockSpec((b, T, LRU_W), lambda t: (0, t, MAIN_LG // LRU_W)),
            pl.BlockSpec((b, T, LRU_W), lambda t: (0, t, MAIN_LX // LRU_W)),
            pl.BlockSpec((CONV_WIDTH, LRU_W), lambda t: (0, 0)),
            vec,
            pl.BlockSpec((rows, LANES, 2 * LANES), lambda t: (0, 0, 0)),
            vec, vec, vec, vec,
        ],
        out_specs=pl.BlockSpec((b, T, LRU_W), lambda t: (0, t, 0)),
        out_shape=jax.ShapeDtypeStruct((b, s, LRU_W), BF16),
        scratch_shapes=[
            pltpu.VMEM((b, SUBLANES, LRU_W), F32),
            pltpu.VMEM((b, rows, LANES), F32),
            pltpu.VMEM((b, T, rows, LANES), F32),
            pltpu.VMEM((b, T, rows, LANES), F32),
        ],
        compiler_params=_params("arbitrary"),
        name="lru",
    )(main3, main3, cw, cb, wab, ba, bi, lam, onorm)


def _outproj_kernel(fox_ref, fn_ref, gla_ref, lru_ref, w_ref, x_ref, o_ref):
    f = fox_ref[...]
    fox_n = (f * _rms_scale(f) * fn_ref[...]).astype(BF16)
    mix = jnp.concatenate([fox_n, gla_ref[...], lru_ref[...]], axis=1)
    o_ref[...] = x_ref[...] + _dot(mix, w_ref[...])


def _outproj(fox2, fnorm, gla2, lru2, w, layer, x2, tm):
    m, d = x2.shape
    dmix = w.shape[1]
    return pl.pallas_call(
        _outproj_kernel,
        grid=(m // tm,),
        in_specs=[
            pl.BlockSpec((tm, FOX_W), lambda i: (i, 0)),
            pl.BlockSpec((1, FOX_W), lambda i: (0, 0)),
            pl.BlockSpec((tm, GLA_W), lambda i: (i, 0)),
            pl.BlockSpec((tm, LRU_W), lambda i: (i, 0)),
            pl.BlockSpec((None, dmix, d), lambda i: (layer, 0, 0)),
            pl.BlockSpec((tm, d), lambda i: (i, 0)),
        ],
        out_specs=pl.BlockSpec((tm, d), lambda i: (i, 0)),
        out_shape=jax.ShapeDtypeStruct((m, d), F32),
        compiler_params=_params("parallel"),
        name="out_proj",
    )(fox2, fnorm, gla2, lru2, w, x2)


def _ffn_kernel(x_ref, g_ref, wg_ref, wu_ref, wd_ref, fg_ref, o_ref, xn_ref, *, final):
    @pl.when(pl.program_id(1) == 0)
    def _():
        x = x_ref[...]
        xn_ref[...] = (x * _rms_scale(x) * g_ref[...]).astype(BF16)
        o_ref[...] = x

    xn = xn_ref[...]
    gate = _dot(xn, wg_ref[...])
    up = _dot(xn, wu_ref[...])
    hid = (gate * jax.nn.sigmoid(gate) * up).astype(BF16)
    o_ref[...] += _dot(hid, wd_ref[...])

    if final:
        @pl.when(pl.program_id(1) == pl.num_programs(1) - 1)
        def _():
            y = o_ref[...]
            o_ref[...] = y * _rms_scale(y) * fg_ref[...]


def _ffn(x2, g, wg, wu, wd, layer, final_g, final, tm, th):
    m, d = x2.shape
    hdim = wg.shape[2]
    return pl.pallas_call(
        functools.partial(_ffn_kernel, final=final),
        grid=(m // tm, hdim // th),
        in_specs=[
            pl.BlockSpec((tm, d), lambda i, j: (i, 0)),
            pl.BlockSpec((1, d), lambda i, j: (0, 0)),
            pl.BlockSpec((None, d, th), lambda i, j: (layer, 0, j)),
            pl.BlockSpec((None, d, th), lambda i, j: (layer, 0, j)),
            pl.BlockSpec((None, th, d), lambda i, j: (layer, j, 0)),
            pl.BlockSpec((1, d), lambda i, j: (0, 0)),
        ],
        out_specs=pl.BlockSpec((tm, d), lambda i, j: (i, 0)),
        out_shape=jax.ShapeDtypeStruct((m, d), F32),
        scratch_shapes=[pltpu.VMEM((tm, d), BF16)],
        compiler_params=_params("parallel", "arbitrary"),
        name="ffn",
    )(x2, g, wg, wu, wd, final_g)


def _tile(n, want):
    t = min(want, n)
    while n % t:
        t //= 2
    return t


def _prep_weights(w_in, w2, w_a, w_i):
    depth, d, _ = w_in.shape
    sizes = [FOX_W, FOX_W, FOX_W, FOX_HEADS, GLA_KW, GLA_KW, GLA_W, GLA_W, GLA_RANK, LRU_W, LRU_W]
    offs = [0]
    for sz in sizes:
        offs.append(offs[-1] + sz)
    col = lambda i: w_in[:, :, offs[i]:offs[i + 1]]
    wm = jnp.concatenate([col(0), col(1), col(2), col(4), col(5), col(6), col(7), col(9), col(10)],
                         axis=2).astype(BF16)
    ws = jnp.concatenate(
        [col(3), col(8), jnp.zeros((depth, d, LANES - FOX_HEADS - GLA_RANK), w_in.dtype)],
        axis=2).astype(BF16)
    w2_pad = jnp.zeros((depth, LANES, GLA_KW), F32).at[:, SMALL_GR:SMALL_GR + GLA_RANK, :].set(w2)
    def pair_bd(w):
        w = w.reshape(depth, LRU_W // LANES, 2, LRU_BW, LRU_BW)
        z = jnp.zeros((depth, LRU_W // LANES, LRU_BW, LRU_BW), w.dtype)
        top = jnp.concatenate([w[:, :, 0], z], axis=3)
        bot = jnp.concatenate([z, w[:, :, 1]], axis=3)
        return jnp.concatenate([top, bot], axis=2)
    wab = jnp.concatenate([pair_bd(w_a), pair_bd(w_i)], axis=3).astype(BF16)
    return wm, ws, w2_pad, wab


def kernel(x, norm_mix, w_in, fox_f_bias, fox_out_norm, gla_gate_w2, gla_gate_bias, gla_head_norm,
           conv_w, conv_b, lru_w_a, lru_b_a, lru_w_i, lru_b_i, lru_lambda, lru_out_norm, w_out,
           norm_ffn, w_gate, w_up, w_down, final_norm):
    b, s, d = x.shape
    depth = w_in.shape[0]
    m = b * s
    x2 = x.reshape(m, d)
    tm_big = _tile(m, 1024)
    row = lambda v: v.reshape(1, -1)

    wm, ws, w2_pad, wab = _prep_weights(w_in, gla_gate_w2, lru_w_a, lru_w_i)
    w_out16, w_gate16, w_up16, w_down16 = (w.astype(BF16) for w in (w_out, w_gate, w_up, w_down))
    fbias = jnp.zeros((depth, 1, LANES), F32).at[:, 0, :FOX_HEADS].set(fox_f_bias)

    for l in range(depth):
        main, small = _inproj(x2, row(norm_mix[l]), wm, ws, l, tm_big, _tile(MAIN_W, 1024))
        main3 = main.reshape(b, s, MAIN_W)
        small3 = small.reshape(b, s, LANES)

        fox = _fox(main3, _foxprep(small3, fbias[l]))

        gla = _gla(main3, small3, w2_pad[l], row(gla_gate_bias[l]), row(gla_head_norm[l]))

        lru = _lru(main3, conv_w[l], row(conv_b[l]), wab[l], row(lru_b_a[l]), row(lru_b_i[l]),
                   row(lru_lambda[l]), row(lru_out_norm[l]))

        x2 = _outproj(fox.reshape(m, FOX_W), row(fox_out_norm[l]), gla.reshape(m, GLA_W),
                      lru.reshape(m, LRU_W), w_out16, l, x2, _tile(m, 512))

        x2 = _ffn(x2, row(norm_ffn[l]), w_gate16, w_up16, w_down16, l, row(final_norm),
                  l == depth - 1, tm_big, _tile(w_gate.shape[2], 512))

    return x2.reshape(b, s, d)
```

```python
import functools

import jax
import jax.numpy as jnp
from jax import lax
from jax.experimental import pallas as pl
from jax.experimental.pallas import tpu as pltpu

F32 = jnp.float32
BF16 = jnp.bfloat16

FOX_HEADS = 8
FOX_HD = 64
FOX_W = FOX_HEADS * FOX_HD
GLA_HEADS = 4
GLA_DK = 64
GLA_DV = 128
GLA_KW = GLA_HEADS * GLA_DK
GLA_W = GLA_HEADS * GLA_DV
GLA_RANK = 16
GLA_TAU = 16.0
LRU_BLOCKS = 16
LRU_BW = 64
LRU_W = LRU_BLOCKS * LRU_BW
LRU_C = 8.0
CONV_WIDTH = 4
RMS_EPS = 1e-6

LANES = 128
SUBLANES = 8
VMEM_LIMIT = 56 * 1024 * 1024

MAIN_FQ = 0
MAIN_FK = FOX_W
MAIN_FV = 2 * FOX_W
MAIN_GQ = 3 * FOX_W
MAIN_GK = MAIN_GQ + GLA_KW
MAIN_GV = MAIN_GK + GLA_KW
MAIN_GG = MAIN_GV + GLA_W
MAIN_LG = MAIN_GG + GLA_W
MAIN_LX = MAIN_LG + LRU_W
MAIN_W = MAIN_LX + LRU_W
SMALL_FF = 0
SMALL_GR = FOX_HEADS


def _dot(a, b):
    return jnp.dot(a, b, preferred_element_type=F32)


def _dot_nt(a, b):
    return lax.dot_general(a, b, (((1,), (1,)), ((), ())), preferred_element_type=F32)


def _dot_tn(a, b):
    return lax.dot_general(a, b, (((0,), (0,)), ((), ())), preferred_element_type=F32)


def _split3(x):
    hi = x.astype(BF16)
    r1 = x - hi.astype(F32)
    mid = r1.astype(BF16)
    lo = (r1 - mid.astype(F32)).astype(BF16)
    return hi, mid, lo


def _tri_cumsum(tri, x):
    hi, mid, lo = _split3(x)
    return _dot(tri, hi) + _dot(tri, mid) + _dot(tri, lo)


def _lower_tri(n):
    r = lax.broadcasted_iota(jnp.int32, (n, n), 0)
    c = lax.broadcasted_iota(jnp.int32, (n, n), 1)
    return jnp.where(r >= c, 1.0, 0.0).astype(BF16)


def _log_sigmoid(x):
    return jnp.minimum(x, 0.0) - jnp.log1p(jnp.exp(-jnp.abs(x)))


def _rms_scale(x):
    return lax.rsqrt(jnp.mean(x * x, axis=-1, keepdims=True) + RMS_EPS)


def _params(*sem):
    return pltpu.CompilerParams(dimension_semantics=sem, vmem_limit_bytes=VMEM_LIMIT)


def _inproj_kernel(x_ref, g_ref, wm_ref, ws_ref, main_ref, small_ref, xn_ref):
    @pl.when(pl.program_id(1) == 0)
    def _():
        x = x_ref[...]
        xn = (x * _rms_scale(x) * g_ref[...]).astype(BF16)
        xn_ref[...] = xn
        small_ref[...] = _dot(xn, ws_ref[...])

    main_ref[...] = _dot(xn_ref[...], wm_ref[...]).astype(BF16)


def _inproj(x2, g, wm, ws, layer, tm, tn):
    m, d = x2.shape
    n = wm.shape[2]
    return pl.pallas_call(
        _inproj_kernel,
        grid=(m // tm, n // tn),
        in_specs=[
            pl.BlockSpec((tm, d), lambda i, j: (i, 0)),
            pl.BlockSpec((1, d), lambda i, j: (0, 0)),
            pl.BlockSpec((None, d, tn), lambda i, j: (layer, 0, j)),
            pl.BlockSpec((None, d, LANES), lambda i, j: (layer, 0, 0)),
        ],
        out_specs=[
            pl.BlockSpec((tm, tn), lambda i, j: (i, j)),
            pl.BlockSpec((tm, LANES), lambda i, j: (i, 0)),
        ],
        out_shape=[
            jax.ShapeDtypeStruct((m, n), BF16),
            jax.ShapeDtypeStruct((m, LANES), F32),
        ],
        scratch_shapes=[pltpu.VMEM((tm, d), BF16)],
        compiler_params=_params("parallel", "arbitrary"),
        name="in_proj",
    )(x2, g, wm, ws)


def _foxprep_kernel(small_ref, bias_ref, ccol_ref, *, blk):
    s = small_ref.shape[0]
    tri = _lower_tri(blk)
    carry = jnp.zeros((1, LANES), F32)
    for i in range(s // blk):
        x = small_ref[i * blk:(i + 1) * blk, :] + bias_ref[...]
        c = _tri_cumsum(tri, _log_sigmoid(x)) + carry
        carry = c[blk - 1:blk, :]
        ccol_ref[i * blk:(i + 1) * blk, :] = c


def _foxprep(small3, bias_pad):
    b, s, _ = small3.shape
    blk = min(256, s)
    return pl.pallas_call(
        functools.partial(_foxprep_kernel, blk=blk),
        grid=(b,),
        in_specs=[
            pl.BlockSpec((None, s, LANES), lambda i: (i, 0, 0)),
            pl.BlockSpec((1, LANES), lambda i: (0, 0)),
        ],
        out_specs=pl.BlockSpec((None, s, LANES), lambda i: (i, 0, 0)),
        out_shape=jax.ShapeDtypeStruct((b, s, LANES), F32),
        compiler_params=_params("parallel"),
        name="fox_prep",
    )(small3, bias_pad)


LOG2E = 1.4426950408889634


def _fox_kernel(q_ref, k_ref, v_ref, ccol_ref, o_ref, vt_ref, kx_ref, qx_ref, *, tq):
    s = q_ref.shape[0]
    pair = pl.program_id(1)
    lane = lax.broadcasted_iota(jnp.int32, (1, LANES), 1)
    first = lane < FOX_HD

    vt_ref[...] = v_ref[...].astype(F32).T.astype(BF16)
    cc = ccol_ref[...]
    for hh in range(2):
        col = jnp.sum(jnp.where(lane == 2 * pair + hh, cc, 0.0), axis=1, keepdims=True)
        hi, mid, lo = (t.astype(F32) for t in _split3(jnp.broadcast_to(col * LOG2E, (s, LANES))))
        kx = jnp.where(lane < 3, 1.0, jnp.where(lane == 3, -hi, jnp.where(lane == 4, -mid,
                       jnp.where(lane == 5, -lo, 0.0))))
        qx = jnp.where(lane == 0, hi, jnp.where(lane == 1, mid, jnp.where(lane == 2, lo,
                       jnp.where(lane < 6, 1.0, 0.0))))
        kx_ref[hh] = kx.astype(BF16)
        qx_ref[hh] = qx.astype(BF16)

    kpos = lax.broadcasted_iota(jnp.int32, (tq, tq), 0)
    qpos = lax.broadcasted_iota(jnp.int32, (tq, tq), 1)
    causal = kpos <= qpos
    top_half = lax.broadcasted_iota(jnp.int32, (LANES, 1), 0) < FOX_HD

    for i in range(s // tq):
        q0, p_len = i * tq, (i + 1) * tq
        q = (q_ref[q0:p_len, :].astype(F32) * (FOX_HD ** -0.5 * LOG2E)).astype(BF16)
        o_heads = []
        for hh in range(2):
            sel = first if hh == 0 else jnp.logical_not(first)
            qa = jnp.concatenate([jnp.where(sel, q, jnp.zeros_like(q)), qx_ref[hh, q0:p_len, :]], axis=1)
            ka = jnp.concatenate([k_ref[0:p_len, :], kx_ref[hh, 0:p_len, :]], axis=1)
            st = _dot_nt(ka, qa)
            diag = jnp.where(causal, st[q0:p_len, :], -jnp.inf)
            st = diag if i == 0 else jnp.concatenate([st[0:q0, :], diag], axis=0)
            m = jnp.max(st, axis=0, keepdims=True)
            p = jnp.exp2(st - m)
            l = jnp.sum(p, axis=0, keepdims=True)
            ot = _dot(vt_ref[:, 0:p_len], p.astype(BF16))
            o_heads.append(ot * (1.0 / l))
        o_ref[q0:p_len, :] = jnp.where(top_half, o_heads[0], o_heads[1]).T


def _fox(main3, ccol, tq=512):
    b, s, _ = main3.shape
    tq = min(tq, s)
    npair = FOX_W // LANES
    return pl.pallas_call(
        functools.partial(_fox_kernel, tq=tq),
        grid=(b, npair),
        in_specs=[
            pl.BlockSpec((None, s, LANES), lambda i, p: (i, 0, MAIN_FQ // LANES + p)),
            pl.BlockSpec((None, s, LANES), lambda i, p: (i, 0, MAIN_FK // LANES + p)),
            pl.BlockSpec((None, s, LANES), lambda i, p: (i, 0, MAIN_FV // LANES + p)),
            pl.BlockSpec((None, s, LANES), lambda i, p: (i, 0, 0)),
        ],
        out_specs=pl.BlockSpec((None, s, LANES), lambda i, p: (i, 0, p)),
        out_shape=jax.ShapeDtypeStruct((b, s, FOX_W), F32),
        scratch_shapes=[pltpu.VMEM((LANES, s), BF16), pltpu.VMEM((2, s, LANES), BF16),
                        pltpu.VMEM((2, s, LANES), BF16)],
        compiler_params=_params("parallel", "arbitrary"),
        name="fox",
    )(main3, main3, main3, ccol)


def _gla_kernel(q_ref, k_ref, v_ref, g_ref, small_ref, w2_ref, gb_ref, hn_ref, o_ref, state_ref):
    @pl.when(pl.program_id(0) == 0)
    def _():
        state_ref[...] = jnp.zeros_like(state_ref)

    for bi in range(q_ref.shape[0]):
        _gla_chunk(q_ref.at[bi], k_ref.at[bi], v_ref.at[bi], g_ref.at[bi], small_ref.at[bi],
                   w2_ref, gb_ref, hn_ref, o_ref.at[bi], state_ref.at[bi])


def _gla_chunk(q_ref, k_ref, v_ref, g_ref, small_ref, w2_ref, gb_ref, hn_ref, o_ref, state_ref):
    L = q_ref.shape[0]

    sm_hi, sm_mid, _ = _split3(small_ref[...])
    w_hi, w_mid, _ = _split3(w2_ref[...])
    z = _dot(sm_hi, w_hi) + _dot(sm_mid, w_hi) + _dot(sm_hi, w_mid) + gb_ref[...]
    la = _log_sigmoid(z) * (1.0 / GLA_TAU)
    bcum = _tri_cumsum(_lower_tri(L), la)
    b_last = bcum[L - 1:L, :]
    b_mid = bcum[L // 2 - 1:L // 2, :]

    q = q_ref[...].astype(F32) * (GLA_DK ** -0.5)
    k = k_ref[...].astype(F32)
    q_in = (q * jnp.exp(bcum)).astype(BF16)
    q_mid = q * jnp.exp(bcum - b_mid)
    k_mid = (k * jnp.exp(b_mid - bcum)).astype(BF16)
    k_st = (k * jnp.exp(b_last - bcum)).astype(BF16)
    v = v_ref[...]

    state_t = state_ref[...]
    o_inter = _dot_nt(q_in, state_t.astype(BF16))

    lane = lax.broadcasted_iota(jnp.int32, (1, GLA_KW), 1)
    r = lax.broadcasted_iota(jnp.int32, (L, L), 0)
    c = lax.broadcasted_iota(jnp.int32, (L, L), 1)
    causal = r >= c
    hn = hn_ref[...]
    outs = []
    for h in range(GLA_HEADS):
        in_head = (lane >= h * GLA_DK) & (lane < (h + 1) * GLA_DK)
        qh = jnp.where(in_head, q_mid, 0.0).astype(BF16)
        a = jnp.where(causal, _dot_nt(qh, k_mid), 0.0)
        vs = slice(h * GLA_DV, (h + 1) * GLA_DV)
        o = _dot(a.astype(BF16), v[:, vs]) + o_inter[:, vs]
        y = o * _rms_scale(o) * hn
        gate = g_ref[:, vs].astype(F32)
        outs.append(y * (gate * jax.nn.sigmoid(gate)))
    o_ref[...] = jnp.concatenate(outs, axis=1).astype(BF16)

    upd = _dot_tn(v, k_st)
    rr = lax.broadcasted_iota(jnp.int32, upd.shape, 0) // GLA_DV
    cc = lax.broadcasted_iota(jnp.int32, upd.shape, 1) // GLA_DK
    state_ref[...] = jnp.where(rr == cc, state_t * jnp.exp(b_last) + upd, 0.0)


def _gla(main3, small3, w2_pad, gbias, hnorm, chunk=128):
    b, s, _ = main3.shape
    L = min(chunk, s)
    return pl.pallas_call(
        _gla_kernel,
        grid=(s // L,),
        in_specs=[
            pl.BlockSpec((b, L, GLA_KW), lambda c: (0, c, MAIN_GQ // GLA_KW)),
            pl.BlockSpec((b, L, GLA_KW), lambda c: (0, c, MAIN_GK // GLA_KW)),
            pl.BlockSpec((b, L, GLA_W), lambda c: (0, c, MAIN_GV // GLA_W)),
            pl.BlockSpec((b, L, GLA_W), lambda c: (0, c, MAIN_GG // GLA_W)),
            pl.BlockSpec((b, L, LANES), lambda c: (0, c, 0)),
            pl.BlockSpec((LANES, GLA_KW), lambda c: (0, 0)),
            pl.BlockSpec((1, GLA_KW), lambda c: (0, 0)),
            pl.BlockSpec((1, GLA_DV), lambda c: (0, 0)),
        ],
        out_specs=pl.BlockSpec((b, L, GLA_W), lambda c: (0, c, 0)),
        out_shape=jax.ShapeDtypeStruct((b, s, GLA_W), BF16),
        scratch_shapes=[pltpu.VMEM((b, GLA_W, GLA_KW), F32)],
        compiler_params=_params("arbitrary"),
        name="gla",
    )(main3, main3, main3, main3, small3, w2_pad, gbias, hnorm)


def _gelu_tanh(x):
    return 0.5 * x * (1.0 + jnp.tanh(0.7978845608028654 * (x + 0.044715 * (x * x * x))))


def _lru_kernel(lg_ref, lx_ref, cw_ref, cb_ref, wab_ref, ba_ref, bi_ref, lam_ref, on_ref,
                o_ref, prev_ref, h_ref, a_ref, u_ref):
    nb, T, _ = lx_ref.shape
    rows = LRU_W // LANES

    @pl.when(pl.program_id(0) == 0)
    def _():
        prev_ref[...] = jnp.zeros_like(prev_ref)
        h_ref[...] = jnp.zeros_like(h_ref)

    cw = cw_ref[...]
    lam_ls = _log_sigmoid(lam_ref[...])
    for bi in range(nb):
        x = lx_ref[bi].astype(F32)
        xcat = jnp.concatenate([prev_ref[bi], x], axis=0)
        xc = cb_ref[...] + cw[CONV_WIDTH - 1:CONV_WIDTH, :] * x
        for d in range(1, CONV_WIDTH):
            shifted = pltpu.roll(xcat, d, 0)[SUBLANES:, :]
            xc = xc + cw[CONV_WIDTH - 1 - d:CONV_WIDTH - d, :] * shifted
        prev_ref[bi] = x[T - SUBLANES:, :]

        xcb = xc.astype(BF16)
        zr, zi = [], []
        for g in range(rows):
            z = _dot(xcb[:, g * LANES:(g + 1) * LANES], wab_ref[g])
            zr.append(z[:, :LANES])
            zi.append(z[:, LANES:])
        r = jax.nn.sigmoid(jnp.concatenate(zr, axis=1) + ba_ref[...])
        ig = jax.nn.sigmoid(jnp.concatenate(zi, axis=1) + bi_ref[...])
        a = jnp.exp((LRU_C * r) * lam_ls)
        u = jnp.sqrt(1.0 - a * a) * (ig * xc)
        a_ref[bi] = a.reshape(T, rows, LANES)
        u_ref[bi] = u.reshape(T, rows, LANES)

    def step(t, hs):
        out = []
        for bi in range(nb):
            h = a_ref[bi, t] * hs[bi] + u_ref[bi, t]
            u_ref[bi, t] = h
            out.append(h)
        return tuple(out)

    hs = lax.fori_loop(0, T, step, tuple(h_ref[bi] for bi in range(nb)), unroll=8)
    for bi in range(nb):
        h_ref[bi] = hs[bi]

    for bi in range(nb):
        y = u_ref[bi].reshape(T, LRU_W) * _gelu_tanh(lg_ref[bi].astype(F32))
        o_ref[bi] = (y * _rms_scale(y) * on_ref[...]).astype(BF16)


def _lru(main3, cw, cb, wab, ba, bi, lam, onorm, tile=256):
    b, s, _ = main3.shape
    T = min(tile, s)
    rows = LRU_W // LANES
    vec = pl.BlockSpec((1, LRU_W), lambda t: (0, 0))
    return pl.pallas_call(
        _lru_kernel,
        grid=(s // T,),
        in_specs=[
            pl.BlockSpec((b, T, LRU_W), lambda t: (0, t, MAIN_LG // LRU_W)),
            pl.BlockSpec((b, T, LRU_W), lambda t: (0, t, MAIN_LX // LRU_W)),
            pl.BlockSpec((CONV_WIDTH, LRU_W), lambda t: (0, 0)),
            vec,
            pl.BlockSpec((rows, LANES, 2 * LANES), lambda t: (0, 0, 0)),
            vec, vec, vec, vec,
        ],
        out_specs=pl.BlockSpec((b, T, LRU_W), lambda t: (0, t, 0)),
        out_shape=jax.ShapeDtypeStruct((b, s, LRU_W), BF16),
        scratch_shapes=[
            pltpu.VMEM((b, SUBLANES, LRU_W), F32),
            pltpu.VMEM((b, rows, LANES), F32),
            pltpu.VMEM((b, T, rows, LANES), F32),
            pltpu.VMEM((b, T, rows, LANES), F32),
        ],
        compiler_params=_params("arbitrary"),
        name="lru",
    )(main3, main3, cw, cb, wab, ba, bi, lam, onorm)


def _outproj_kernel(fox_ref, fn_ref, gla_ref, lru_ref, w_ref, x_ref, o_ref):
    f = fox_ref[...]
    fox_n = (f * _rms_scale(f) * fn_ref[...]).astype(BF16)
    mix = jnp.concatenate([fox_n, gla_ref[...], lru_ref[...]], axis=1)
    o_ref[...] = x_ref[...] + _dot(mix, w_ref[...])


def _outproj(fox2, fnorm, gla2, lru2, w, layer, x2, tm):
    m, d = x2.shape
    dmix = w.shape[1]
    return pl.pallas_call(
        _outproj_kernel,
        grid=(m // tm,),
        in_specs=[
            pl.BlockSpec((tm, FOX_W), lambda i: (i, 0)),
            pl.BlockSpec((1, FOX_W), lambda i: (0, 0)),
            pl.BlockSpec((tm, GLA_W), lambda i: (i, 0)),
            pl.BlockSpec((tm, LRU_W), lambda i: (i, 0)),
            pl.BlockSpec((None, dmix, d), lambda i: (layer, 0, 0)),
            pl.BlockSpec((tm, d), lambda i: (i, 0)),
        ],
        out_specs=pl.BlockSpec((tm, d), lambda i: (i, 0)),
        out_shape=jax.ShapeDtypeStruct((m, d), F32),
        compiler_params=_params("parallel"),
        name="out_proj",
    )(fox2, fnorm, gla2, lru2, w, x2)


def _ffn_kernel(x_ref, g_ref, wg_ref, wu_ref, wd_ref, fg_ref, o_ref, xn_ref, *, final):
    @pl.when(pl.program_id(1) == 0)
    def _():
        x = x_ref[...]
        xn_ref[...] = (x * _rms_scale(x) * g_ref[...]).astype(BF16)
        o_ref[...] = x

    xn = xn_ref[...]
    gate = _dot(xn, wg_ref[...])
    up = _dot(xn, wu_ref[...])
    hid = (gate * jax.nn.sigmoid(gate) * up).astype(BF16)
    o_ref[...] += _dot(hid, wd_ref[...])

    if final:
        @pl.when(pl.program_id(1) == pl.num_programs(1) - 1)
        def _():
            y = o_ref[...]
            o_ref[...] = y * _rms_scale(y) * fg_ref[...]


def _ffn(x2, g, wg, wu, wd, layer, final_g, final, tm, th):
    m, d = x2.shape
    hdim = wg.shape[2]
    return pl.pallas_call(
        functools.partial(_ffn_kernel, final=final),
        grid=(m // tm, hdim // th),
        in_specs=[
            pl.BlockSpec((tm, d), lambda i, j: (i, 0)),
            pl.BlockSpec((1, d), lambda i, j: (0, 0)),
            pl.BlockSpec((None, d, th), lambda i, j: (layer, 0, j)),
            pl.BlockSpec((None, d, th), lambda i, j: (layer, 0, j)),
            pl.BlockSpec((None, th, d), lambda i, j: (layer, j, 0)),
            pl.BlockSpec((1, d), lambda i, j: (0, 0)),
        ],
        out_specs=pl.BlockSpec((tm, d), lambda i, j: (i, 0)),
        out_shape=jax.ShapeDtypeStruct((m, d), F32),
        scratch_shapes=[pltpu.VMEM((tm, d), BF16)],
        compiler_params=_params("parallel", "arbitrary"),
        name="ffn",
    )(x2, g, wg, wu, wd, final_g)


def _tile(n, want):
    t = min(want, n)
    while n % t:
        t //= 2
    return t


def _prep_weights(w_in, w2, w_a, w_i):
    depth, d, _ = w_in.shape
    sizes = [FOX_W, FOX_W, FOX_W, FOX_HEADS, GLA_KW, GLA_KW, GLA_W, GLA_W, GLA_RANK, LRU_W, LRU_W]
    offs = [0]
    for sz in sizes:
        offs.append(offs[-1] + sz)
    col = lambda i: w_in[:, :, offs[i]:offs[i + 1]]
    wm = jnp.concatenate([col(0), col(1), col(2), col(4), col(5), col(6), col(7), col(9), col(10)],
                         axis=2).astype(BF16)
    ws = jnp.concatenate(
        [col(3), col(8), jnp.zeros((depth, d, LANES - FOX_HEADS - GLA_RANK), w_in.dtype)],
        axis=2).astype(BF16)
    w2_pad = jnp.zeros((depth, LANES, GLA_KW), F32).at[:, SMALL_GR:SMALL_GR + GLA_RANK, :].set(w2)
    def pair_bd(w):
        w = w.reshape(depth, LRU_W // LANES, 2, LRU_BW, LRU_BW)
        z = jnp.zeros((depth, LRU_W // LANES, LRU_BW, LRU_BW), w.dtype)
        top = jnp.concatenate([w[:, :, 0], z], axis=3)
        bot = jnp.concatenate([z, w[:, :, 1]], axis=3)
        return jnp.concatenate([top, bot], axis=2)
    wab = jnp.concatenate([pair_bd(w_a), pair_bd(w_i)], axis=3).astype(BF16)
    return wm, ws, w2_pad, wab


def kernel(x, norm_mix, w_in, fox_f_bias, fox_out_norm, gla_gate_w2, gla_gate_bias, gla_head_norm,
           conv_w, conv_b, lru_w_a, lru_b_a, lru_w_i, lru_b_i, lru_lambda, lru_out_norm, w_out,
           norm_ffn, w_gate, w_up, w_down, final_norm):
    b, s, d = x.shape
    depth = w_in.shape[0]
    m = b * s
    x2 = x.reshape(m, d)
    tm_big = _tile(m, 1024)
    row = lambda v: v.reshape(1, -1)

    wm, ws, w2_pad, wab = _prep_weights(w_in, gla_gate_w2, lru_w_a, lru_w_i)
    w_out16, w_gate16, w_up16, w_down16 = (w.astype(BF16) for w in (w_out, w_gate, w_up, w_down))
    fbias = jnp.zeros((depth, 1, LANES), F32).at[:, 0, :FOX_HEADS].set(fox_f_bias)

    for l in range(depth):
        main, small = _inproj(x2, row(norm_mix[l]), wm, ws, l, tm_big, _tile(MAIN_W, 1024))
        main3 = main.reshape(b, s, MAIN_W)
        small3 = small.reshape(b, s, LANES)

        fox = _fox(main3, _foxprep(small3, fbias[l]))

        gla = _gla(main3, small3, w2_pad[l], row(gla_gate_bias[l]), row(gla_head_norm[l]))

        lru = _lru(main3, conv_w[l], row(conv_b[l]), wab[l], row(lru_b_a[l]), row(lru_b_i[l]),
                   row(lru_lambda[l]), row(lru_out_norm[l]))

        x2 = _outproj(fox.reshape(m, FOX_W), row(fox_out_norm[l]), gla.reshape(m, GLA_W),
                      lru.reshape(m, LRU_W), w_out16, l, x2, _tile(m, 512))

        x2 = _ffn(x2, row(norm_ffn[l]), w_gate16, w_up16, w_down16, l, row(final_norm),
                  l == depth - 1, tm_big, _tile(w_gate.shape[2], 512))

    return x2.reshape(b, s, d)
```

```python
import functools

import jax
import jax.numpy as jnp
from jax import lax
from jax.experimental import pallas as pl
from jax.experimental.pallas import tpu as pltpu

F32 = jnp.float32
BF16 = jnp.bfloat16

FOX_HEADS = 8
FOX_HD = 64
FOX_W = FOX_HEADS * FOX_HD
GLA_HEADS = 4
GLA_DK = 64
GLA_DV = 128
GLA_KW = GLA_HEADS * GLA_DK
GLA_W = GLA_HEADS * GLA_DV
GLA_RANK = 16
GLA_TAU = 16.0
LRU_BLOCKS = 16
LRU_BW = 64
LRU_W = LRU_BLOCKS * LRU_BW
LRU_C = 8.0
CONV_WIDTH = 4
RMS_EPS = 1e-6

LANES = 128
SUBLANES = 8
VMEM_LIMIT = 56 * 1024 * 1024

MAIN_FQ = 0
MAIN_FK = FOX_W
MAIN_FV = 2 * FOX_W
MAIN_GQ = 3 * FOX_W
MAIN_GK = MAIN_GQ + GLA_KW
MAIN_GV = MAIN_GK + GLA_KW
MAIN_GG = MAIN_GV + GLA_W
MAIN_LG = MAIN_GG + GLA_W
MAIN_LX = MAIN_LG + LRU_W
MAIN_W = MAIN_LX + LRU_W
SMALL_FF = 0
SMALL_GR = FOX_HEADS


def _dot(a, b):
    return jnp.dot(a, b, preferred_element_type=F32)


def _dot_nt(a, b):
    return lax.dot_general(a, b, (((1,), (1,)), ((), ())), preferred_element_type=F32)


def _dot_tn(a, b):
    return lax.dot_general(a, b, (((0,), (0,)), ((), ())), preferred_element_type=F32)


def _split3(x):
    hi = x.astype(BF16)
    r1 = x - hi.astype(F32)
    mid = r1.astype(BF16)
    lo = (r1 - mid.astype(F32)).astype(BF16)
    return hi, mid, lo


def _tri_cumsum(tri, x):
    hi, mid, lo = _split3(x)
    return _dot(tri, hi) + _dot(tri, mid) + _dot(tri, lo)


def _lower_tri(n):
    r = lax.broadcasted_iota(jnp.int32, (n, n), 0)
    c = lax.broadcasted_iota(jnp.int32, (n, n), 1)
    return jnp.where(r >= c, 1.0, 0.0).astype(BF16)


def _log_sigmoid(x):
    return jnp.minimum(x, 0.0) - jnp.log1p(jnp.exp(-jnp.abs(x)))


def _rms_scale(x):
    return lax.rsqrt(jnp.mean(x * x, axis=-1, keepdims=True) + RMS_EPS)


def _params(*sem):
    return pltpu.CompilerParams(dimension_semantics=sem, vmem_limit_bytes=VMEM_LIMIT)


def _inproj_kernel(x_ref, g_ref, wm_ref, ws_ref, main_ref, small_ref, xn_ref):
    @pl.when(pl.program_id(1) == 0)
    def _():
        x = x_ref[...]
        xn = (x * _rms_scale(x) * g_ref[...]).astype(BF16)
        xn_ref[...] = xn
        small_ref[...] = _dot(xn, ws_ref[...])

    main_ref[...] = _dot(xn_ref[...], wm_ref[...]).astype(BF16)


def _inproj(x2, g, wm, ws, layer, tm, tn):
    m, d = x2.shape
    n = wm.shape[2]
    return pl.pallas_call(
        _inproj_kernel,
        grid=(m // tm, n // tn),
        in_specs=[
            pl.BlockSpec((tm, d), lambda i, j: (i, 0)),
            pl.BlockSpec((1, d), lambda i, j: (0, 0)),
            pl.BlockSpec((None, d, tn), lambda i, j: (layer, 0, j)),
            pl.BlockSpec((None, d, LANES), lambda i, j: (layer, 0, 0)),
        ],
        out_specs=[
            pl.BlockSpec((tm, tn), lambda i, j: (i, j)),
            pl.BlockSpec((tm, LANES), lambda i, j: (i, 0)),
        ],
        out_shape=[
            jax.ShapeDtypeStruct((m, n), BF16),
            jax.ShapeDtypeStruct((m, LANES), F32),
        ],
        scratch_shapes=[pltpu.VMEM((tm, d), BF16)],
        compiler_params=_params("parallel", "arbitrary"),
        name="in_proj",
    )(x2, g, wm, ws)


def _foxprep_kernel(small_ref, bias_ref, ccol_ref, *, blk):
    s = small_ref.shape[0]
    tri = _lower_tri(blk)
    carry = jnp.zeros((1, LANES), F32)
    for i in range(s // blk):
        x = small_ref[i * blk:(i + 1) * blk, :] + bias_ref[...]
        c = _tri_cumsum(tri, _log_sigmoid(x)) + carry
        carry = c[blk - 1:blk, :]
        ccol_ref[i * blk:(i + 1) * blk, :] = c


def _foxprep(small3, bias_pad):
    b, s, _ = small3.shape
    blk = min(256, s)
    return pl.pallas_call(
        functools.partial(_foxprep_kernel, blk=blk),
        grid=(b,),
        in_specs=[
            pl.BlockSpec((None, s, LANES), lambda i: (i, 0, 0)),
            pl.BlockSpec((1, LANES), lambda i: (0, 0)),
        ],
        out_specs=pl.BlockSpec((None, s, LANES), lambda i: (i, 0, 0)),
        out_shape=jax.ShapeDtypeStruct((b, s, LANES), F32),
        compiler_params=_params("parallel"),
        name="fox_prep",
    )(small3, bias_pad)


LOG2E = 1.4426950408889634


def _fox_kernel(q_ref, k_ref, v_ref, ccol_ref, o_ref, vt_ref, kx_ref, qx_ref, *, tq):
    s = q_ref.shape[0]
    pair = pl.program_id(1)
    lane = lax.broadcasted_iota(jnp.int32, (1, LANES), 1)
    first = lane < FOX_HD

    vt_ref[...] = v_ref[...].astype(F32).T.astype(BF16)
    cc = ccol_ref[...]
    low3, low6 = lane < 3, lane < 6
    term = lane - jnp.where(low3, 0, 3)
    for hh in range(2):
        col = jnp.sum(jnp.where(lane == 2 * pair + hh, cc, 0.0), axis=1, keepdims=True)
        c2 = jnp.broadcast_to(col * LOG2E, (s, LANES))
        hi = c2.astype(BF16).astype(F32)
        r1 = c2 - hi
        mid = r1.astype(BF16).astype(F32)
        t = jnp.where(term == 0, hi, jnp.where(term == 1, mid, r1 - mid))
        kx_ref[hh] = jnp.where(low3, 1.0, jnp.where(low6, -t, 0.0)).astype(BF16)
        qx_ref[hh] = jnp.where(low3, t, jnp.where(low6, 1.0, 0.0)).astype(BF16)

    kpos = lax.broadcasted_iota(jnp.int32, (tq, tq), 0)
    qpos = lax.broadcasted_iota(jnp.int32, (tq, tq), 1)
    causal = kpos <= qpos
    top_half = lax.broadcasted_iota(jnp.int32, (LANES, 1), 0) < FOX_HD

    for i in range(s // tq):
        q0, p_len = i * tq, (i + 1) * tq
        q = (q_ref[q0:p_len, :].astype(F32) * (FOX_HD ** -0.5 * LOG2E)).astype(BF16)
        o_heads = []
        for hh in range(2):
            sel = first if hh == 0 else jnp.logical_not(first)
            qa = jnp.concatenate([jnp.where(sel, q, jnp.zeros_like(q)), qx_ref[hh, q0:p_len, :]], axis=1)
            ka = jnp.concatenate([k_ref[0:p_len, :], kx_ref[hh, 0:p_len, :]], axis=1)
            st = _dot_nt(ka, qa)
            diag = jnp.where(causal, st[q0:p_len, :], -jnp.inf)
            st = diag if i == 0 else jnp.concatenate([st[0:q0, :], diag], axis=0)
            m = jnp.max(st, axis=0, keepdims=True)
            p = jnp.exp2(st - m)
            l = jnp.sum(p, axis=0, keepdims=True)
            ot = _dot(vt_ref[:, 0:p_len], p.astype(BF16))
            o_heads.append(ot * (1.0 / l))
        o_ref[q0:p_len, :] = jnp.where(top_half, o_heads[0], o_heads[1]).T


def _fox(main3, ccol, tq=512):
    b, s, _ = main3.shape
    tq = min(tq, s)
    npair = FOX_W // LANES
    return pl.pallas_call(
        functools.partial(_fox_kernel, tq=tq),
        grid=(b, npair),
        in_specs=[
            pl.BlockSpec((None, s, LANES), lambda i, p: (i, 0, MAIN_FQ // LANES + p)),
            pl.BlockSpec((None, s, LANES), lambda i, p: (i, 0, MAIN_FK // LANES + p)),
            pl.BlockSpec((None, s, LANES), lambda i, p: (i, 0, MAIN_FV // LANES + p)),
            pl.BlockSpec((None, s, LANES), lambda i, p: (i, 0, 0)),
        ],
        out_specs=pl.BlockSpec((None, s, LANES), lambda i, p: (i, 0, p)),
        out_shape=jax.ShapeDtypeStruct((b, s, FOX_W), F32),
        scratch_shapes=[pltpu.VMEM((LANES, s), BF16), pltpu.VMEM((2, s, LANES), BF16),
                        pltpu.VMEM((2, s, LANES), BF16)],
        compiler_params=_params("parallel", "arbitrary"),
        name="fox",
    )(main3, main3, main3, ccol)


def _gla_kernel(q_ref, k_ref, v_ref, g_ref, small_ref, w2_ref, gb_ref, hn_ref, o_ref, state_ref,
                *, chunk):
    @pl.when(pl.program_id(0) == 0)
    def _():
        state_ref[...] = jnp.zeros_like(state_ref)

    for ci in range(q_ref.shape[1] // chunk):
        rows = pl.ds(ci * chunk, chunk)
        for bi in range(q_ref.shape[0]):
            _gla_chunk(q_ref.at[bi, rows], k_ref.at[bi, rows], v_ref.at[bi, rows], g_ref.at[bi, rows],
                       small_ref.at[bi, rows], w2_ref, gb_ref, hn_ref, o_ref.at[bi, rows],
                       state_ref.at[bi])


def _gla_chunk(q_ref, k_ref, v_ref, g_ref, small_ref, w2_ref, gb_ref, hn_ref, o_ref, state_ref):
    L = q_ref.shape[0]

    sm_hi, sm_mid, _ = _split3(small_ref[...])
    w_hi, w_mid, _ = _split3(w2_ref[...])
    z = _dot(sm_hi, w_hi) + _dot(sm_mid, w_hi) + _dot(sm_hi, w_mid) + gb_ref[...]
    la = _log_sigmoid(z) * (1.0 / GLA_TAU)
    bcum = _tri_cumsum(_lower_tri(L), la)
    b_last = bcum[L - 1:L, :]
    b_mid = bcum[L // 2 - 1:L // 2, :]

    q = q_ref[...].astype(F32) * (GLA_DK ** -0.5)
    k = k_ref[...].astype(F32)
    q_in = (q * jnp.exp(bcum)).astype(BF16)
    q_mid = q * jnp.exp(bcum - b_mid)
    k_mid = (k * jnp.exp(b_mid - bcum)).astype(BF16)
    k_st = (k * jnp.exp(b_last - bcum)).astype(BF16)
    v = v_ref[...]

    state_t = state_ref[...]
    o_inter = _dot_nt(q_in, state_t.astype(BF16))

    lane = lax.broadcasted_iota(jnp.int32, (1, GLA_KW), 1)
    r = lax.broadcasted_iota(jnp.int32, (L, L), 0)
    c = lax.broadcasted_iota(jnp.int32, (L, L), 1)
    causal = r >= c
    hn = hn_ref[...]
    outs = []
    for h in range(GLA_HEADS):
        in_head = (lane >= h * GLA_DK) & (lane < (h + 1) * GLA_DK)
        qh = jnp.where(in_head, q_mid, 0.0).astype(BF16)
        a = jnp.where(causal, _dot_nt(qh, k_mid), 0.0)
        vs = slice(h * GLA_DV, (h + 1) * GLA_DV)
        o = _dot(a.astype(BF16), v[:, vs]) + o_inter[:, vs]
        y = o * _rms_scale(o) * hn
        gate = g_ref[:, vs].astype(F32)
        outs.append(y * (gate * jax.nn.sigmoid(gate)))
    o_ref[...] = jnp.concatenate(outs, axis=1).astype(BF16)

    upd = _dot_tn(v, k_st)
    rr = lax.broadcasted_iota(jnp.int32, upd.shape, 0) // GLA_DV
    cc = lax.broadcasted_iota(jnp.int32, upd.shape, 1) // GLA_DK
    state_ref[...] = jnp.where(rr == cc, state_t * jnp.exp(b_last) + upd, 0.0)


def _gla(main3, small3, w2_pad, gbias, hnorm, chunk=128, chunks_per_step=2):
    b, s, _ = main3.shape
    chunk = min(chunk, s)
    L = min(chunk * chunks_per_step, s)
    return pl.pallas_call(
        functools.partial(_gla_kernel, chunk=chunk),
        grid=(s // L,),
        in_specs=[
            pl.BlockSpec((b, L, GLA_KW), lambda c: (0, c, MAIN_GQ // GLA_KW)),
            pl.BlockSpec((b, L, GLA_KW), lambda c: (0, c, MAIN_GK // GLA_KW)),
            pl.BlockSpec((b, L, GLA_W), lambda c: (0, c, MAIN_GV // GLA_W)),
            pl.BlockSpec((b, L, GLA_W), lambda c: (0, c, MAIN_GG // GLA_W)),
            pl.BlockSpec((b, L, LANES), lambda c: (0, c, 0)),
            pl.BlockSpec((LANES, GLA_KW), lambda c: (0, 0)),
            pl.BlockSpec((1, GLA_KW), lambda c: (0, 0)),
            pl.BlockSpec((1, GLA_DV), lambda c: (0, 0)),
        ],
        out_specs=pl.BlockSpec((b, L, GLA_W), lambda c: (0, c, 0)),
        out_shape=jax.ShapeDtypeStruct((b, s, GLA_W), BF16),
        scratch_shapes=[pltpu.VMEM((b, GLA_W, GLA_KW), F32)],
        compiler_params=_params("arbitrary"),
        name="gla",
    )(main3, main3, main3, main3, small3, w2_pad, gbias, hnorm)


def _gelu_tanh(x):
    return 0.5 * x * (1.0 + jnp.tanh(0.7978845608028654 * (x + 0.044715 * (x * x * x))))


def _lru_kernel(lg_ref, lx_ref, cw_ref, cb_ref, wab_ref, ba_ref, bi_ref, lam_ref, on_ref,
                o_ref, prev_ref, h_ref, a_ref, u_ref):
    nb, T, _ = lx_ref.shape
    rows = LRU_W // LANES

    @pl.when(pl.program_id(0) == 0)
    def _():
        prev_ref[...] = jnp.zeros_like(prev_ref)
        h_ref[...] = jnp.zeros_like(h_ref)

    cw = cw_ref[...]
    lam_ls = _log_sigmoid(lam_ref[...])
    for bi in range(nb):
        x = lx_ref[bi].astype(F32)
        xcat = jnp.concatenate([prev_ref[bi], x], axis=0)
        xc = cb_ref[...] + cw[CONV_WIDTH - 1:CONV_WIDTH, :] * x
        for d in range(1, CONV_WIDTH):
            shifted = pltpu.roll(xcat, d, 0)[SUBLANES:, :]
            xc = xc + cw[CONV_WIDTH - 1 - d:CONV_WIDTH - d, :] * shifted
        prev_ref[bi] = x[T - SUBLANES:, :]

        xcb = xc.astype(BF16)
        zr, zi = [], []
        for g in range(rows):
            z = _dot(xcb[:, g * LANES:(g + 1) * LANES], wab_ref[g])
            zr.append(z[:, :LANES])
            zi.append(z[:, LANES:])
        r = jax.nn.sigmoid(jnp.concatenate(zr, axis=1) + ba_ref[...])
        ig = jax.nn.sigmoid(jnp.concatenate(zi, axis=1) + bi_ref[...])
        a = jnp.exp((LRU_C * r) * lam_ls)
        u = jnp.sqrt(1.0 - a * a) * (ig * xc)
        a_ref[bi] = a.reshape(T, rows, LANES)
        u_ref[bi] = u.reshape(T, rows, LANES)

    def step(t, hs):
        out = []
        for bi in range(nb):
            h = a_ref[bi, t] * hs[bi] + u_ref[bi, t]
            u_ref[bi, t] = h
            out.append(h)
        return tuple(out)

    hs = lax.fori_loop(0, T, step, tuple(h_ref[bi] for bi in range(nb)), unroll=8)
    for bi in range(nb):
        h_ref[bi] = hs[bi]

    for bi in range(nb):
        y = u_ref[bi].reshape(T, LRU_W) * _gelu_tanh(lg_ref[bi].astype(F32))
        o_ref[bi] = (y * _rms_scale(y) * on_ref[...]).astype(BF16)


def _lru(main3, cw, cb, wab, ba, bi, lam, onorm, tile=256):
    b, s, _ = main3.shape
    T = min(tile, s)
    rows = LRU_W // LANES
    vec = pl.BlockSpec((1, LRU_W), lambda t: (0, 0))
    return pl.pallas_call(
        _lru_kernel,
        grid=(s // T,),
        in_specs=[
            pl.BlockSpec((b, T, LRU_W), lambda t: (0, t, MAIN_LG // LRU_W)),
            pl.BlockSpec((b, T, LRU_W), lambda t: (0, t, MAIN_LX // LRU_W)),
            pl.BlockSpec((CONV_WIDTH, LRU_W), lambda t: (0, 0)),
            vec,
            pl.BlockSpec((rows, LANES, 2 * LANES), lambda t: (0, 0, 0)),
            vec, vec, vec, vec,
        ],
        out_specs=pl.BlockSpec((b, T, LRU_W), lambda t: (0, t, 0)),
        out_shape=jax.ShapeDtypeStruct((b, s, LRU_W), BF16),
        scratch_shapes=[
            pltpu.VMEM((b, SUBLANES, LRU_W), F32),
            pltpu.VMEM((b, rows, LANES), F32),
            pltpu.VMEM((b, T, rows, LANES), F32),
            pltpu.VMEM((b, T, rows, LANES), F32),
        ],
        compiler_params=_params("arbitrary"),
        name="lru",
    )(main3, main3, cw, cb, wab, ba, bi, lam, onorm)


def _outproj_kernel(fox_ref, fn_ref, gla_ref, lru_ref, w_ref, x_ref, o_ref):
    f = fox_ref[...]
    fox_n = (f * _rms_scale(f) * fn_ref[...]).astype(BF16)
    mix = jnp.concatenate([fox_n, gla_ref[...], lru_ref[...]], axis=1)
    o_ref[...] = x_ref[...] + _dot(mix, w_ref[...])


def _outproj(fox2, fnorm, gla2, lru2, w, layer, x2, tm):
    m, d = x2.shape
    dmix = w.shape[1]
    return pl.pallas_call(
        _outproj_kernel,
        grid=(m // tm,),
        in_specs=[
            pl.BlockSpec((tm, FOX_W), lambda i: (i, 0)),
            pl.BlockSpec((1, FOX_W), lambda i: (0, 0)),
            pl.BlockSpec((tm, GLA_W), lambda i: (i, 0)),
            pl.BlockSpec((tm, LRU_W), lambda i: (i, 0)),
            pl.BlockSpec((None, dmix, d), lambda i: (layer, 0, 0)),
            pl.BlockSpec((tm, d), lambda i: (i, 0)),
        ],
        out_specs=pl.BlockSpec((tm, d), lambda i: (i, 0)),
        out_shape=jax.ShapeDtypeStruct((m, d), F32),
        compiler_params=_params("parallel"),
        name="out_proj",
    )(fox2, fnorm, gla2, lru2, w, x2)


def _ffn_kernel(x_ref, g_ref, wg_ref, wu_ref, wd_ref, fg_ref, o_ref, xn_ref, *, final):
    @pl.when(pl.program_id(1) == 0)
    def _():
        x = x_ref[...]
        xn_ref[...] = (x * _rms_scale(x) * g_ref[...]).astype(BF16)
        o_ref[...] = x

    xn = xn_ref[...]
    gate = _dot(xn, wg_ref[...])
    up = _dot(xn, wu_ref[...])
    hid = (gate * jax.nn.sigmoid(gate) * up).astype(BF16)
    o_ref[...] += _dot(hid, wd_ref[...])

    if final:
        @pl.when(pl.program_id(1) == pl.num_programs(1) - 1)
        def _():
            y = o_ref[...]
            o_ref[...] = y * _rms_scale(y) * fg_ref[...]


def _ffn(x2, g, wg, wu, wd, layer, final_g, final, tm, th):
    m, d = x2.shape
    hdim = wg.shape[2]
    return pl.pallas_call(
        functools.partial(_ffn_kernel, final=final),
        grid=(m // tm, hdim // th),
        in_specs=[
            pl.BlockSpec((tm, d), lambda i, j: (i, 0)),
            pl.BlockSpec((1, d), lambda i, j: (0, 0)),
            pl.BlockSpec((None, d, th), lambda i, j: (layer, 0, j)),
            pl.BlockSpec((None, d, th), lambda i, j: (layer, 0, j)),
            pl.BlockSpec((None, th, d), lambda i, j: (layer, j, 0)),
            pl.BlockSpec((1, d), lambda i, j: (0, 0)),
        ],
        out_specs=pl.BlockSpec((tm, d), lambda i, j: (i, 0)),
        out_shape=jax.ShapeDtypeStruct((m, d), F32),
        scratch_shapes=[pltpu.VMEM((tm, d), BF16)],
        compiler_params=_params("parallel", "arbitrary"),
        name="ffn",
    )(x2, g, wg, wu, wd, final_g)


def _tile(n, want):
    t = min(want, n)
    while n % t:
        t //= 2
    return t


def _prep_weights(w_in, w2, w_a, w_i):
    depth, d, _ = w_in.shape
    sizes = [FOX_W, FOX_W, FOX_W, FOX_HEADS, GLA_KW, GLA_KW, GLA_W, GLA_W, GLA_RANK, LRU_W, LRU_W]
    offs = [0]
    for sz in sizes:
        offs.append(offs[-1] + sz)
    col = lambda i: w_in[:, :, offs[i]:offs[i + 1]]
    wm = jnp.concatenate([col(0), col(1), col(2), col(4), col(5), col(6), col(7), col(9), col(10)],
                         axis=2).astype(BF16)
    ws = jnp.concatenate(
        [col(3), col(8), jnp.zeros((depth, d, LANES - FOX_HEADS - GLA_RANK), w_in.dtype)],
        axis=2).astype(BF16)
    w2_pad = jnp.zeros((depth, LANES, GLA_KW), F32).at[:, SMALL_GR:SMALL_GR + GLA_RANK, :].set(w2)
    def pair_bd(w):
        w = w.reshape(depth, LRU_W // LANES, 2, LRU_BW, LRU_BW)
        z = jnp.zeros((depth, LRU_W // LANES, LRU_BW, LRU_BW), w.dtype)
        top = jnp.concatenate([w[:, :, 0], z], axis=3)
        bot = jnp.concatenate([z, w[:, :, 1]], axis=3)
        return jnp.concatenate([top, bot], axis=2)
    wab = jnp.concatenate([pair_bd(w_a), pair_bd(w_i)], axis=3).astype(BF16)
    return wm, ws, w2_pad, wab


def kernel(x, norm_mix, w_in, fox_f_bias, fox_out_norm, gla_gate_w2, gla_gate_bias, gla_head_norm,
           conv_w, conv_b, lru_w_a, lru_b_a, lru_w_i, lru_b_i, lru_lambda, lru_out_norm, w_out,
           norm_ffn, w_gate, w_up, w_down, final_norm):
    b, s, d = x.shape
    depth = w_in.shape[0]
    m = b * s
    x2 = x.reshape(m, d)
    tm_big = _tile(m, 1024)
    row = lambda v: v.reshape(1, -1)

    wm, ws, w2_pad, wab = _prep_weights(w_in, gla_gate_w2, lru_w_a, lru_w_i)
    w_out16, w_gate16, w_up16, w_down16 = (w.astype(BF16) for w in (w_out, w_gate, w_up, w_down))
    fbias = jnp.zeros((depth, 1, LANES), F32).at[:, 0, :FOX_HEADS].set(fox_f_bias)

    for l in range(depth):
        main, small = _inproj(x2, row(norm_mix[l]), wm, ws, l, tm_big, _tile(MAIN_W, 1024))
        main3 = main.reshape(b, s, MAIN_W)
        small3 = small.reshape(b, s, LANES)

        fox = _fox(main3, _foxprep(small3, fbias[l]))

        gla = _gla(main3, small3, w2_pad[l], row(gla_gate_bias[l]), row(gla_head_norm[l]))

        lru = _lru(main3, conv_w[l], row(conv_b[l]), wab[l], row(lru_b_a[l]), row(lru_b_i[l]),
                   row(lru_lambda[l]), row(lru_out_norm[l]))

        x2 = _outproj(fox.reshape(m, FOX_W), row(fox_out_norm[l]), gla.reshape(m, GLA_W),
                      lru.reshape(m, LRU_W), w_out16, l, x2, _tile(m, 512))

        x2 = _ffn(x2, row(norm_ffn[l]), w_gate16, w_up16, w_down16, l, row(final_norm),
                  l == depth - 1, tm_big, _tile(w_gate.shape[2], 512))

    return x2.reshape(b, s, d)
```

```python
import functools

import jax
import jax.numpy as jnp
from jax import lax
from jax.experimental import pallas as pl
from jax.experimental.pallas import tpu as pltpu

F32 = jnp.float32
BF16 = jnp.bfloat16

FOX_HEADS = 8
FOX_HD = 64
FOX_W = FOX_HEADS * FOX_HD
GLA_HEADS = 4
GLA_DK = 64
GLA_DV = 128
GLA_KW = GLA_HEADS * GLA_DK
GLA_W = GLA_HEADS * GLA_DV
GLA_RANK = 16
GLA_TAU = 16.0
LRU_BLOCKS = 16
LRU_BW = 64
LRU_W = LRU_BLOCKS * LRU_BW
LRU_C = 8.0
CONV_WIDTH = 4
RMS_EPS = 1e-6

LANES = 128
SUBLANES = 8
VMEM_LIMIT = 56 * 1024 * 1024

MAIN_FQ = 0
MAIN_FK = FOX_W
MAIN_FV = 2 * FOX_W
MAIN_GQ = 3 * FOX_W
MAIN_GK = MAIN_GQ + GLA_KW
MAIN_GV = MAIN_GK + GLA_KW
MAIN_GG = MAIN_GV + GLA_W
MAIN_LG = MAIN_GG + GLA_W
MAIN_LX = MAIN_LG + LRU_W
MAIN_W = MAIN_LX + LRU_W
SMALL_FF = 0
SMALL_GR = FOX_HEADS


def _dot(a, b):
    return jnp.dot(a, b, preferred_element_type=F32)


def _dot_nt(a, b):
    return lax.dot_general(a, b, (((1,), (1,)), ((), ())), preferred_element_type=F32)


def _dot_tn(a, b):
    return lax.dot_general(a, b, (((0,), (0,)), ((), ())), preferred_element_type=F32)


def _split3(x):
    hi = x.astype(BF16)
    r1 = x - hi.astype(F32)
    mid = r1.astype(BF16)
    lo = (r1 - mid.astype(F32)).astype(BF16)
    return hi, mid, lo


def _tri_cumsum(tri, x):
    hi, mid, lo = _split3(x)
    return _dot(tri, hi) + _dot(tri, mid) + _dot(tri, lo)


def _lower_tri(n):
    r = lax.broadcasted_iota(jnp.int32, (n, n), 0)
    c = lax.broadcasted_iota(jnp.int32, (n, n), 1)
    return jnp.where(r >= c, 1.0, 0.0).astype(BF16)


def _log_sigmoid(x):
    return jnp.minimum(x, 0.0) - jnp.log1p(jnp.exp(-jnp.abs(x)))


def _rms_scale(x):
    return lax.rsqrt(jnp.mean(x * x, axis=-1, keepdims=True) + RMS_EPS)


def _params(*sem):
    return pltpu.CompilerParams(dimension_semantics=sem, vmem_limit_bytes=VMEM_LIMIT)


def _inproj_kernel(x_ref, g_ref, wm_ref, ws_ref, main_ref, small_ref, xn_ref):
    @pl.when(pl.program_id(1) == 0)
    def _():
        x = x_ref[...]
        xn = (x * _rms_scale(x) * g_ref[...]).astype(BF16)
        xn_ref[...] = xn
        small_ref[...] = _dot(xn, ws_ref[...])

    main_ref[...] = _dot(xn_ref[...], wm_ref[...]).astype(BF16)


def _inproj(x2, g, wm, ws, layer, tm, tn):
    m, d = x2.shape
    n = wm.shape[2]
    return pl.pallas_call(
        _inproj_kernel,
        grid=(m // tm, n // tn),
        in_specs=[
            pl.BlockSpec((tm, d), lambda i, j: (i, 0)),
            pl.BlockSpec((1, d), lambda i, j: (0, 0)),
            pl.BlockSpec((None, d, tn), lambda i, j: (layer, 0, j)),
            pl.BlockSpec((None, d, LANES), lambda i, j: (layer, 0, 0)),
        ],
        out_specs=[
            pl.BlockSpec((tm, tn), lambda i, j: (i, j)),
            pl.BlockSpec((tm, LANES), lambda i, j: (i, 0)),
        ],
        out_shape=[
            jax.ShapeDtypeStruct((m, n), BF16),
            jax.ShapeDtypeStruct((m, LANES), F32),
        ],
        scratch_shapes=[pltpu.VMEM((tm, d), BF16)],
        compiler_params=_params("parallel", "arbitrary"),
        name="in_proj",
    )(x2, g, wm, ws)


def _foxprep_kernel(small_ref, bias_ref, ccol_ref, *, blk):
    s = small_ref.shape[0]
    tri = _lower_tri(blk)
    carry = jnp.zeros((1, LANES), F32)
    for i in range(s // blk):
        x = small_ref[i * blk:(i + 1) * blk, :] + bias_ref[...]
        c = _tri_cumsum(tri, _log_sigmoid(x)) + carry
        carry = c[blk - 1:blk, :]
        ccol_ref[i * blk:(i + 1) * blk, :] = c


def _foxprep(small3, bias_pad):
    b, s, _ = small3.shape
    blk = min(256, s)
    return pl.pallas_call(
        functools.partial(_foxprep_kernel, blk=blk),
        grid=(b,),
        in_specs=[
            pl.BlockSpec((None, s, LANES), lambda i: (i, 0, 0)),
            pl.BlockSpec((1, LANES), lambda i: (0, 0)),
        ],
        out_specs=pl.BlockSpec((None, s, LANES), lambda i: (i, 0, 0)),
        out_shape=jax.ShapeDtypeStruct((b, s, LANES), F32),
        compiler_params=_params("parallel"),
        name="fox_prep",
    )(small3, bias_pad)


LOG2E = 1.4426950408889634


def _fox_kernel(q_ref, k_ref, v_ref, ccol_ref, o_ref, vt_ref, kx_ref, qx_ref, *, tq):
    s = q_ref.shape[0]
    pair = pl.program_id(1)
    lane = lax.broadcasted_iota(jnp.int32, (1, LANES), 1)
    first = lane < FOX_HD

    vt_ref[...] = v_ref[...].astype(F32).T.astype(BF16)
    cc = ccol_ref[...]
    low3, low6 = lane < 3, lane < 6
    term = lane - jnp.where(low3, 0, 3)
    for hh in range(2):
        col = jnp.sum(jnp.where(lane == 2 * pair + hh, cc, 0.0), axis=1, keepdims=True)
        c2 = jnp.broadcast_to(col * LOG2E, (s, LANES))
        hi = c2.astype(BF16).astype(F32)
        r1 = c2 - hi
        mid = r1.astype(BF16).astype(F32)
        t = jnp.where(term == 0, hi, jnp.where(term == 1, mid, r1 - mid))
        kx_ref[hh] = jnp.where(low3, 1.0, jnp.where(low6, -t, 0.0)).astype(BF16)
        qx_ref[hh] = jnp.where(low3, t, jnp.where(low6, 1.0, 0.0)).astype(BF16)

    kpos = lax.broadcasted_iota(jnp.int32, (tq, tq), 0)
    qpos = lax.broadcasted_iota(jnp.int32, (tq, tq), 1)
    causal = kpos <= qpos
    top_half = lax.broadcasted_iota(jnp.int32, (LANES, 1), 0) < FOX_HD

    for i in range(s // tq):
        q0, p_len = i * tq, (i + 1) * tq
        q = (q_ref[q0:p_len, :].astype(F32) * (FOX_HD ** -0.5 * LOG2E)).astype(BF16)
        o_heads = []
        for hh in range(2):
            sel = first if hh == 0 else jnp.logical_not(first)
            qa = jnp.concatenate([jnp.where(sel, q, jnp.zeros_like(q)), qx_ref[hh, q0:p_len, :]], axis=1)
            ka = jnp.concatenate([k_ref[0:p_len, :], kx_ref[hh, 0:p_len, :]], axis=1)
            st = _dot_nt(ka, qa)
            diag = jnp.where(causal, st[q0:p_len, :], -jnp.inf)
            st = diag if i == 0 else jnp.concatenate([st[0:q0, :], diag], axis=0)
            m = jnp.max(st, axis=0, keepdims=True)
            p = jnp.exp2(st - m)
            l = jnp.sum(p, axis=0, keepdims=True)
            ot = _dot(vt_ref[:, 0:p_len], p.astype(BF16))
            o_heads.append(ot * (1.0 / l))
        o_ref[q0:p_len, :] = jnp.where(top_half, o_heads[0], o_heads[1]).T.astype(BF16)


def _fox(main3, ccol, tq=512):
    b, s, _ = main3.shape
    tq = min(tq, s)
    npair = FOX_W // LANES
    return pl.pallas_call(
        functools.partial(_fox_kernel, tq=tq),
        grid=(b, npair),
        in_specs=[
            pl.BlockSpec((None, s, LANES), lambda i, p: (i, 0, MAIN_FQ // LANES + p)),
            pl.BlockSpec((None, s, LANES), lambda i, p: (i, 0, MAIN_FK // LANES + p)),
            pl.BlockSpec((None, s, LANES), lambda i, p: (i, 0, MAIN_FV // LANES + p)),
            pl.BlockSpec((None, s, LANES), lambda i, p: (i, 0, 0)),
        ],
        out_specs=pl.BlockSpec((None, s, LANES), lambda i, p: (i, 0, p)),
        out_shape=jax.ShapeDtypeStruct((b, s, FOX_W), BF16),
        scratch_shapes=[pltpu.VMEM((LANES, s), BF16), pltpu.VMEM((2, s, LANES), BF16),
                        pltpu.VMEM((2, s, LANES), BF16)],
        compiler_params=_params("parallel", "arbitrary"),
        name="fox",
    )(main3, main3, main3, ccol)


def _gla_kernel(q_ref, k_ref, v_ref, g_ref, small_ref, w2_ref, gb_ref, hn_ref, o_ref, state_ref,
                *, chunk):
    @pl.when(pl.program_id(0) == 0)
    def _():
        state_ref[...] = jnp.zeros_like(state_ref)

    for ci in range(q_ref.shape[1] // chunk):
        rows = pl.ds(ci * chunk, chunk)
        for bi in range(q_ref.shape[0]):
            _gla_chunk(q_ref.at[bi, rows], k_ref.at[bi, rows], v_ref.at[bi, rows], g_ref.at[bi, rows],
                       small_ref.at[bi, rows], w2_ref, gb_ref, hn_ref, o_ref.at[bi, rows],
                       state_ref.at[bi])


def _gla_chunk(q_ref, k_ref, v_ref, g_ref, small_ref, w2_ref, gb_ref, hn_ref, o_ref, state_ref):
    L = q_ref.shape[0]

    sm_hi, sm_mid, _ = _split3(small_ref[...])
    w_hi, w_mid, _ = _split3(w2_ref[...])
    z = _dot(sm_hi, w_hi) + _dot(sm_mid, w_hi) + _dot(sm_hi, w_mid) + gb_ref[...]
    la = _log_sigmoid(z) * (1.0 / GLA_TAU)
    bcum = _tri_cumsum(_lower_tri(L), la)
    b_last = bcum[L - 1:L, :]
    b_mid = bcum[L // 2 - 1:L // 2, :]

    q = q_ref[...].astype(F32) * (GLA_DK ** -0.5)
    k = k_ref[...].astype(F32)
    q_in = (q * jnp.exp(bcum)).astype(BF16)
    q_mid = q * jnp.exp(bcum - b_mid)
    k_mid = (k * jnp.exp(b_mid - bcum)).astype(BF16)
    k_st = (k * jnp.exp(b_last - bcum)).astype(BF16)
    v = v_ref[...]

    state_t = state_ref[...]
    o_inter = _dot_nt(q_in, state_t.astype(BF16))

    lane = lax.broadcasted_iota(jnp.int32, (1, GLA_KW), 1)
    r = lax.broadcasted_iota(jnp.int32, (L, L), 0)
    c = lax.broadcasted_iota(jnp.int32, (L, L), 1)
    causal = r >= c
    hn = hn_ref[...]
    outs = []
    for h in range(GLA_HEADS):
        in_head = (lane >= h * GLA_DK) & (lane < (h + 1) * GLA_DK)
        qh = jnp.where(in_head, q_mid, 0.0).astype(BF16)
        a = jnp.where(causal, _dot_nt(qh, k_mid), 0.0)
        vs = slice(h * GLA_DV, (h + 1) * GLA_DV)
        o = _dot(a.astype(BF16), v[:, vs]) + o_inter[:, vs]
        y = o * _rms_scale(o) * hn
        gate = g_ref[:, vs].astype(F32)
        outs.append(y * (gate * jax.nn.sigmoid(gate)))
    o_ref[...] = jnp.concatenate(outs, axis=1).astype(BF16)

    upd = _dot_tn(v, k_st)
    rr = lax.broadcasted_iota(jnp.int32, upd.shape, 0) // GLA_DV
    cc = lax.broadcasted_iota(jnp.int32, upd.shape, 1) // GLA_DK
    state_ref[...] = jnp.where(rr == cc, state_t * jnp.exp(b_last) + upd, 0.0)


def _gla(main3, small3, w2_pad, gbias, hnorm, chunk=128, chunks_per_step=2):
    b, s, _ = main3.shape
    chunk = min(chunk, s)
    L = min(chunk * chunks_per_step, s)
    return pl.pallas_call(
        functools.partial(_gla_kernel, chunk=chunk),
        grid=(s // L,),
        in_specs=[
            pl.BlockSpec((b, L, GLA_KW), lambda c: (0, c, MAIN_GQ // GLA_KW)),
            pl.BlockSpec((b, L, GLA_KW), lambda c: (0, c, MAIN_GK // GLA_KW)),
            pl.BlockSpec((b, L, GLA_W), lambda c: (0, c, MAIN_GV // GLA_W)),
            pl.BlockSpec((b, L, GLA_W), lambda c: (0, c, MAIN_GG // GLA_W)),
            pl.BlockSpec((b, L, LANES), lambda c: (0, c, 0)),
            pl.BlockSpec((LANES, GLA_KW), lambda c: (0, 0)),
            pl.BlockSpec((1, GLA_KW), lambda c: (0, 0)),
            pl.BlockSpec((1, GLA_DV), lambda c: (0, 0)),
        ],
        out_specs=pl.BlockSpec((b, L, GLA_W), lambda c: (0, c, 0)),
        out_shape=jax.ShapeDtypeStruct((b, s, GLA_W), BF16),
        scratch_shapes=[pltpu.VMEM((b, GLA_W, GLA_KW), F32)],
        compiler_params=_params("arbitrary"),
        name="gla",
    )(main3, main3, main3, main3, small3, w2_pad, gbias, hnorm)


def _gelu_tanh(x):
    return 0.5 * x * (1.0 + jnp.tanh(0.7978845608028654 * (x + 0.044715 * (x * x * x))))


def _lru_kernel(lg_ref, lx_ref, cw_ref, cb_ref, wab_ref, ba_ref, bi_ref, lam_ref, on_ref,
                o_ref, prev_ref, h_ref, a_ref, u_ref):
    nb, T, _ = lx_ref.shape
    rows = LRU_W // LANES

    @pl.when(pl.program_id(0) == 0)
    def _():
        prev_ref[...] = jnp.zeros_like(prev_ref)
        h_ref[...] = jnp.zeros_like(h_ref)

    cw = cw_ref[...]
    lam_ls = _log_sigmoid(lam_ref[...])
    for bi in range(nb):
        x = lx_ref[bi].astype(F32)
        xcat = jnp.concatenate([prev_ref[bi], x], axis=0)
        xc = cb_ref[...] + cw[CONV_WIDTH - 1:CONV_WIDTH, :] * x
        for d in range(1, CONV_WIDTH):
            shifted = pltpu.roll(xcat, d, 0)[SUBLANES:, :]
            xc = xc + cw[CONV_WIDTH - 1 - d:CONV_WIDTH - d, :] * shifted
        prev_ref[bi] = x[T - SUBLANES:, :]

        xcb = xc.astype(BF16)
        zr, zi = [], []
        for g in range(rows):
            z = _dot(xcb[:, g * LANES:(g + 1) * LANES], wab_ref[g])
            zr.append(z[:, :LANES])
            zi.append(z[:, LANES:])
        r = jax.nn.sigmoid(jnp.concatenate(zr, axis=1) + ba_ref[...])
        ig = jax.nn.sigmoid(jnp.concatenate(zi, axis=1) + bi_ref[...])
        a = jnp.exp((LRU_C * r) * lam_ls)
        u = jnp.sqrt(1.0 - a * a) * (ig * xc)
        a_ref[bi] = a.reshape(T, rows, LANES)
        u_ref[bi] = u.reshape(T, rows, LANES)

    def step(t, hs):
        out = []
        for bi in range(nb):
            h = a_ref[bi, t] * hs[bi] + u_ref[bi, t]
            u_ref[bi, t] = h
            out.append(h)
        return tuple(out)

    hs = lax.fori_loop(0, T, step, tuple(h_ref[bi] for bi in range(nb)), unroll=8)
    for bi in range(nb):
        h_ref[bi] = hs[bi]

    for bi in range(nb):
        y = u_ref[bi].reshape(T, LRU_W) * _gelu_tanh(lg_ref[bi].astype(F32))
        o_ref[bi] = (y * _rms_scale(y) * on_ref[...]).astype(BF16)


def _lru(main3, cw, cb, wab, ba, bi, lam, onorm, tile=256):
    b, s, _ = main3.shape
    T = min(tile, s)
    rows = LRU_W // LANES
    vec = pl.BlockSpec((1, LRU_W), lambda t: (0, 0))
    return pl.pallas_call(
        _lru_kernel,
        grid=(s // T,),
        in_specs=[
            pl.BlockSpec((b, T, LRU_W), lambda t: (0, t, MAIN_LG // LRU_W)),
            pl.BlockSpec((b, T, LRU_W), lambda t: (0, t, MAIN_LX // LRU_W)),
            pl.BlockSpec((CONV_WIDTH, LRU_W), lambda t: (0, 0)),
            vec,
            pl.BlockSpec((rows, LANES, 2 * LANES), lambda t: (0, 0, 0)),
            vec, vec, vec, vec,
        ],
        out_specs=pl.BlockSpec((b, T, LRU_W), lambda t: (0, t, 0)),
        out_shape=jax.ShapeDtypeStruct((b, s, LRU_W), BF16),
        scratch_shapes=[
            pltpu.VMEM((b, SUBLANES, LRU_W), F32),
            pltpu.VMEM((b, rows, LANES), F32),
            pltpu.VMEM((b, T, rows, LANES), F32),
            pltpu.VMEM((b, T, rows, LANES), F32),
        ],
        compiler_params=_params("arbitrary"),
        name="lru",
    )(main3, main3, cw, cb, wab, ba, bi, lam, onorm)


def _outproj_kernel(fox_ref, fn_ref, gla_ref, lru_ref, w_ref, x_ref, o_ref):
    f = fox_ref[...].astype(F32)
    fox_n = (f * _rms_scale(f) * fn_ref[...]).astype(BF16)
    mix = jnp.concatenate([fox_n, gla_ref[...], lru_ref[...]], axis=1)
    o_ref[...] = x_ref[...] + _dot(mix, w_ref[...])


def _outproj(fox2, fnorm, gla2, lru2, w, layer, x2, tm):
    m, d = x2.shape
    dmix = w.shape[1]
    return pl.pallas_call(
        _outproj_kernel,
        grid=(m // tm,),
        in_specs=[
            pl.BlockSpec((tm, FOX_W), lambda i: (i, 0)),
            pl.BlockSpec((1, FOX_W), lambda i: (0, 0)),
            pl.BlockSpec((tm, GLA_W), lambda i: (i, 0)),
            pl.BlockSpec((tm, LRU_W), lambda i: (i, 0)),
            pl.BlockSpec((None, dmix, d), lambda i: (layer, 0, 0)),
            pl.BlockSpec((tm, d), lambda i: (i, 0)),
        ],
        out_specs=pl.BlockSpec((tm, d), lambda i: (i, 0)),
        out_shape=jax.ShapeDtypeStruct((m, d), F32),
        compiler_params=_params("parallel"),
        name="out_proj",
    )(fox2, fnorm, gla2, lru2, w, x2)


def _ffn_kernel(x_ref, g_ref, wg_ref, wu_ref, wd_ref, fg_ref, o_ref, xn_ref, *, final):
    @pl.when(pl.program_id(1) == 0)
    def _():
        x = x_ref[...]
        xn_ref[...] = (x * _rms_scale(x) * g_ref[...]).astype(BF16)
        o_ref[...] = x

    xn = xn_ref[...]
    gate = _dot(xn, wg_ref[...])
    up = _dot(xn, wu_ref[...])
    hid = (gate * jax.nn.sigmoid(gate) * up).astype(BF16)
    o_ref[...] += _dot(hid, wd_ref[...])

    if final:
        @pl.when(pl.program_id(1) == pl.num_programs(1) - 1)
        def _():
            y = o_ref[...]
            o_ref[...] = y * _rms_scale(y) * fg_ref[...]


def _ffn(x2, g, wg, wu, wd, layer, final_g, final, tm, th):
    m, d = x2.shape
    hdim = wg.shape[2]
    return pl.pallas_call(
        functools.partial(_ffn_kernel, final=final),
        grid=(m // tm, hdim // th),
        in_specs=[
            pl.BlockSpec((tm, d), lambda i, j: (i, 0)),
            pl.BlockSpec((1, d), lambda i, j: (0, 0)),
            pl.BlockSpec((None, d, th), lambda i, j: (layer, 0, j)),
            pl.BlockSpec((None, d, th), lambda i, j: (layer, 0, j)),
            pl.BlockSpec((None, th, d), lambda i, j: (layer, j, 0)),
            pl.BlockSpec((1, d), lambda i, j: (0, 0)),
        ],
        out_specs=pl.BlockSpec((tm, d), lambda i, j: (i, 0)),
        out_shape=jax.ShapeDtypeStruct((m, d), F32),
        scratch_shapes=[pltpu.VMEM((tm, d), BF16)],
        compiler_params=_params("parallel", "arbitrary"),
        name="ffn",
    )(x2, g, wg, wu, wd, final_g)


def _tile(n, want):
    t = min(want, n)
    while n % t:
        t //= 2
    return t


def _prep_weights(w_in, w2, w_a, w_i):
    depth, d, _ = w_in.shape
    sizes = [FOX_W, FOX_W, FOX_W, FOX_HEADS, GLA_KW, GLA_KW, GLA_W, GLA_W, GLA_RANK, LRU_W, LRU_W]
    offs = [0]
    for sz in sizes:
        offs.append(offs[-1] + sz)
    col = lambda i: w_in[:, :, offs[i]:offs[i + 1]]
    wm = jnp.concatenate([col(0), col(1), col(2), col(4), col(5), col(6), col(7), col(9), col(10)],
                         axis=2).astype(BF16)
    ws = jnp.concatenate(
        [col(3), col(8), jnp.zeros((depth, d, LANES - FOX_HEADS - GLA_RANK), w_in.dtype)],
        axis=2).astype(BF16)
    w2_pad = jnp.zeros((depth, LANES, GLA_KW), F32).at[:, SMALL_GR:SMALL_GR + GLA_RANK, :].set(w2)
    def pair_bd(w):
        w = w.reshape(depth, LRU_W // LANES, 2, LRU_BW, LRU_BW)
        z = jnp.zeros((depth, LRU_W // LANES, LRU_BW, LRU_BW), w.dtype)
        top = jnp.concatenate([w[:, :, 0], z], axis=3)
        bot = jnp.concatenate([z, w[:, :, 1]], axis=3)
        return jnp.concatenate([top, bot], axis=2)
    wab = jnp.concatenate([pair_bd(w_a), pair_bd(w_i)], axis=3).astype(BF16)
    return wm, ws, w2_pad, wab


def kernel(x, norm_mix, w_in, fox_f_bias, fox_out_norm, gla_gate_w2, gla_gate_bias, gla_head_norm,
           conv_w, conv_b, lru_w_a, lru_b_a, lru_w_i, lru_b_i, lru_lambda, lru_out_norm, w_out,
           norm_ffn, w_gate, w_up, w_down, final_norm):
    b, s, d = x.shape
    depth = w_in.shape[0]
    m = b * s
    x2 = x.reshape(m, d)
    tm_big = _tile(m, 1024)
    row = lambda v: v.reshape(1, -1)

    wm, ws, w2_pad, wab = _prep_weights(w_in, gla_gate_w2, lru_w_a, lru_w_i)
    w_out16, w_gate16, w_up16, w_down16 = (w.astype(BF16) for w in (w_out, w_gate, w_up, w_down))
    fbias = jnp.zeros((depth, 1, LANES), F32).at[:, 0, :FOX_HEADS].set(fox_f_bias)

    for l in range(depth):
        main, small = _inproj(x2, row(norm_mix[l]), wm, ws, l, tm_big, _tile(MAIN_W, 1024))
        main3 = main.reshape(b, s, MAIN_W)
        small3 = small.reshape(b, s, LANES)

        fox = _fox(main3, _foxprep(small3, fbias[l]))

        gla = _gla(main3, small3, w2_pad[l], row(gla_gate_bias[l]), row(gla_head_norm[l]))

        lru = _lru(main3, conv_w[l], row(conv_b[l]), wab[l], row(lru_b_a[l]), row(lru_b_i[l]),
                   row(lru_lambda[l]), row(lru_out_norm[l]))

        x2 = _outproj(fox.reshape(m, FOX_W), row(fox_out_norm[l]), gla.reshape(m, GLA_W),
                      lru.reshape(m, LRU_W), w_out16, l, x2, _tile(m, 512))

        x2 = _ffn(x2, row(norm_ffn[l]), w_gate16, w_up16, w_down16, l, row(final_norm),
                  l == depth - 1, tm_big, _tile(w_gate.shape[2], 512))

    return x2.reshape(b, s, d)
```
